```python
import math
import jax, jax.numpy as jnp
from jax import lax
import numpy as np

D_MODEL = 1024
BATCH = 8
SEQ = 2048
DEPTH = 4
DEC_BATCH = 128
DEC_SEQ = 8
PAST_LEN = 16384
PAGE_SIZE = 128

N_BRANCH = 4
MIX_W = D_MODEL // 4
POOL_WINDOWS = (2, 4, 8, 16)
POOL_GROUPS = len(POOL_WINDOWS)
POOL_GW = MIX_W // POOL_GROUPS
POOL_BUF = max(POOL_WINDOWS) - 1
SSM_GROUP_CH = 16
SSM_GROUPS = MIX_W // SSM_GROUP_CH
SSM_STATE = 64
SC_WIDTH = 3
CF_WIDTH = 31
FF_DIM = -(-8 * D_MODEL // (3 * 256)) * 256
OFF_POOL = 0
OFF_SSM = OFF_POOL + MIX_W
OFF_SC = OFF_SSM + MIX_W
OFF_CF = OFF_SC + 3 * MIX_W
OFF_GATE = OFF_CF + 2 * MIX_W
IN_COLS = OFF_GATE + N_BRANCH * D_MODEL
RMS_EPS = 1e-6
LN_EPS = 1e-5

kernel_name = 'hybrid_pool_s5_conv_gated_decoder_step'

f32 = jnp.float32


def rmsnorm(x, g):
    xf = x.astype(f32)
    y = xf * lax.rsqrt(jnp.mean(xf * xf, axis=-1, keepdims=True) + RMS_EPS)
    return (y * g.astype(f32)).astype(x.dtype)


def causal_dwconv(v, buf, w):
    ext = jnp.concatenate([buf.astype(v.dtype), v], axis=1)
    y = lax.conv_general_dilated(ext, w[:, None, :].astype(v.dtype), window_strides=(1,), padding='VALID',
                                 dimension_numbers=('NWC', 'WIO', 'NWC'), feature_group_count=v.shape[-1])
    return y, ext[:, ext.shape[1] - (w.shape[0] - 1):]


def pool_mixer(v, buf, pos0, w, scale):
    L = v.shape[1]
    ext = jnp.concatenate([buf.astype(v.dtype), v], axis=1)
    extf = ext.astype(f32)
    cs = jnp.concatenate([jnp.zeros_like(extf[:, :1]), jnp.cumsum(extf, axis=1)], axis=1)
    pos = jnp.arange(L, dtype=f32) + pos0
    outs = []
    for k, win in enumerate(POOL_WINDOWS):
        lo, hi = k * POOL_GW, (k + 1) * POOL_GW
        s = cs[:, POOL_BUF + 1:POOL_BUF + 1 + L, lo:hi] - cs[:, POOL_BUF + 1 - win:POOL_BUF + 1 - win + L, lo:hi]
        cnt = jnp.minimum(pos + 1.0, float(win))
        d = s / cnt[None, :, None] - extf[:, POOL_BUF:, lo:hi]
        outs.append(jnp.einsum('nlc,cd->nld', d, w[k].astype(f32)))
    out = jnp.concatenate(outs, axis=-1) * scale.astype(f32)
    return out.astype(v.dtype), ext[:, ext.shape[1] - POOL_BUF:]


def _ssm_combine(e1, e2):
    a1, b1 = e1
    a2, b2 = e2
    return a1 * a2, a2 * b1 + b2


def ssm_mixer(u, h_re, h_im, lam_re, lam_im, log_dt, b_re, b_im, c_re, c_im, d_skip, w_glu, b_glu):
    N, L, _ = u.shape
    uf = u.astype(f32).reshape(N, L, SSM_GROUPS, SSM_GROUP_CH)
    lam = lam_re.astype(f32) + 1j * lam_im.astype(f32)
    dt = jnp.exp(log_dt.astype(f32))[:, None]
    lam_bar = jnp.exp(lam * dt)
    b_bar = ((lam_bar - 1.0) / lam)[..., None] * (b_re.astype(f32) + 1j * b_im.astype(f32))
    c = c_re.astype(f32) + 1j * c_im.astype(f32)
    bu = jnp.einsum('nlgc,gpc->nlgp', uf.astype(jnp.complex64), b_bar)
    h0 = h_re.astype(f32) + 1j * h_im.astype(f32)
    bu = bu.at[:, 0].add(lam_bar[None] * h0)
    a = jnp.broadcast_to(lam_bar, (1, L) + lam_bar.shape)
    _, h = lax.associative_scan(_ssm_combine, (a, bu), axis=1)
    y = jnp.einsum('nlgp,gcp->nlgc', h, c).real + d_skip.astype(f32).reshape(SSM_GROUPS, SSM_GROUP_CH) * uf
    z = jax.nn.gelu(y.reshape(N, L, MIX_W))
    out = z * jax.nn.sigmoid(z @ w_glu.astype(f32) + b_glu.astype(f32))
    h_last = h[:, -1]
    return out.astype(u.dtype), h_last.real, h_last.imag


def shortconv_mixer(bg, cg, hx, buf, w):
    y, nb = causal_dwconv(cg * hx, buf, w)
    return bg * y, nb


def conformer_mixer(a, b, buf, w, ln_g, ln_b):
    v = a * jax.nn.sigmoid(b)
    y, nb = causal_dwconv(v, buf, w)
    yf = y.astype(f32)
    mu = jnp.mean(yf, axis=-1, keepdims=True)
    var = jnp.mean(jnp.square(yf - mu), axis=-1, keepdims=True)
    yn = (yf - mu) * lax.rsqrt(var + LN_EPS) * ln_g.astype(f32) + ln_b.astype(f32)
    return jax.nn.silu(yn).astype(v.dtype), nb


def trunk(x, pos0, st_pool, st_re, st_im, st_sc, st_cf, weights):
    (norm1_g, norm2_g, final_g, w_in, b_gate, pool_w, pool_scale, lam_re, lam_im, log_dt,
     b_re, b_im, c_re, c_im, d_skip, w_glu, b_glu, sc_w, cf_w, cf_ln_g, cf_ln_b,
     w_branch, w_out, w_ffn_in, w_ffn_out) = weights
    N, L = x.shape[0], x.shape[1]
    n_pool, n_re, n_im, n_sc, n_cf = [], [], [], [], []
    for i in range(DEPTH):
        h = rmsnorm(x, norm1_g[i])
        proj = jnp.einsum('nld,de->nle', h, w_in[i])
        ya, pb = pool_mixer(proj[..., OFF_POOL:OFF_SSM], st_pool[i], pos0, pool_w[i], pool_scale[i])
        yb, hr, hi = ssm_mixer(proj[..., OFF_SSM:OFF_SC], st_re[i], st_im[i], lam_re[i], lam_im[i], log_dt[i],
                               b_re[i], b_im[i], c_re[i], c_im[i], d_skip[i], w_glu[i], b_glu[i])
        yc, sb = shortconv_mixer(proj[..., OFF_SC:OFF_SC + MIX_W], proj[..., OFF_SC + MIX_W:OFF_SC + 2 * MIX_W],
                                 proj[..., OFF_SC + 2 * MIX_W:OFF_CF], st_sc[i], sc_w[i])
        yd, cb = conformer_mixer(proj[..., OFF_CF:OFF_CF + MIX_W], proj[..., OFF_CF + MIX_W:OFF_GATE],
                                 st_cf[i], cf_w[i], cf_ln_g[i], cf_ln_b[i])
        gates = jax.nn.sigmoid(proj[..., OFF_GATE:].reshape(N, L, N_BRANCH, D_MODEL) + b_gate[i])
        branches = jnp.stack([ya, yb, yc, yd], axis=2)
        bproj = jnp.einsum('nlkc,kcd->nlkd', branches, w_branch[i])
        merged = jnp.sum(gates * bproj, axis=2)
        x = x + merged @ w_out[i]
        h2 = rmsnorm(x, norm2_g[i])
        g, up = jnp.split(h2 @ w_ffn_in[i], 2, axis=-1)
        x = x + (jax.nn.silu(g) * up) @ w_ffn_out[i]
        n_pool.append(pb); n_re.append(hr); n_im.append(hi); n_sc.append(sb); n_cf.append(cb)
    return (rmsnorm(x, final_g), jnp.stack(n_pool), jnp.stack(n_re), jnp.stack(n_im),
            jnp.stack(n_sc), jnp.stack(n_cf))


def setup_inputs(seed: int = 0) -> dict:
    key = jax.random.key(seed)
    ks = jax.random.split(key, 40)
    nrm = lambda k, shape, s: jax.random.normal(k, shape, f32) * s
    D, C = D_MODEL, MIX_W
    G, P, GC = SSM_GROUPS, SSM_STATE, SSM_GROUP_CH
    lam_im0 = jnp.broadcast_to(math.pi * jnp.arange(P, dtype=f32), (DEPTH, G, P))
    return {
        'x_prompt': nrm(ks[0], (BATCH, SEQ, D), 1.0),
        'x_sample': nrm(ks[1], (DEC_BATCH, DEC_SEQ, D), 1.0),
        'state_pool': nrm(ks[2], (DEPTH, DEC_BATCH, POOL_BUF, C), 1.0),
        'state_ssm_re': nrm(ks[3], (DEPTH, DEC_BATCH, G, P), 0.5),
        'state_ssm_im': nrm(ks[4], (DEPTH, DEC_BATCH, G, P), 0.5),
        'state_shortconv': nrm(ks[5], (DEPTH, DEC_BATCH, SC_WIDTH - 1, C), 1.0),
        'state_conformer': nrm(ks[6], (DEPTH, DEC_BATCH, CF_WIDTH - 1, C), 0.5),
        'norm1_g': 1.0 + nrm(ks[7], (DEPTH, D), 0.01),
        'norm2_g': 1.0 + nrm(ks[8], (DEPTH, D), 0.01),
        'final_g': 1.0 + nrm(ks[9], (D,), 0.01),
        'w_in': nrm(ks[10], (DEPTH, D, IN_COLS), D ** -0.5),
        'b_gate': nrm(ks[11], (DEPTH, N_BRANCH, D), 0.01),
        'pool_w': nrm(ks[12], (DEPTH, POOL_GROUPS, POOL_GW, POOL_GW), POOL_GW ** -0.5),
        'pool_scale': 1.0 + nrm(ks[13], (DEPTH, C), 0.1),
        'lam_re': -0.5 + nrm(ks[14], (DEPTH, G, P), 0.01),
        'lam_im': lam_im0 + nrm(ks[15], (DEPTH, G, P), 0.01),
        'log_dt': jax.random.uniform(ks[16], (DEPTH, G), f32, math.log(1e-3), math.log(1e-1)),
        'b_re': nrm(ks[17], (DEPTH, G, P, GC), (2 * GC) ** -0.5),
        'b_im': nrm(ks[18], (DEPTH, G, P, GC), (2 * GC) ** -0.5),
        'c_re': nrm(ks[19], (DEPTH, G, GC, P), (2 * P) ** -0.5),
        'c_im': nrm(ks[20], (DEPTH, G, GC, P), (2 * P) ** -0.5),
        'd_skip': nrm(ks[21], (DEPTH, C), 1.0),
        'w_glu': nrm(ks[22], (DEPTH, C, C), C ** -0.5),
        'b_glu': nrm(ks[23], (DEPTH, C), 0.01),
        'sc_w': nrm(ks[24], (DEPTH, SC_WIDTH, C), SC_WIDTH ** -0.5),
        'cf_w': nrm(ks[25], (DEPTH, CF_WIDTH, C), CF_WIDTH ** -0.5),
        'cf_ln_g': 1.0 + nrm(ks[26], (DEPTH, C), 0.01),
        'cf_ln_b': nrm(ks[27], (DEPTH, C), 0.01),
        'w_branch': nrm(ks[28], (DEPTH, N_BRANCH, C, D), C ** -0.5),
        'w_out': nrm(ks[29], (DEPTH, D, D), D ** -0.5),
        'w_ffn_in': nrm(ks[30], (DEPTH, D, 2 * FF_DIM), D ** -0.5),
        'w_ffn_out': nrm(ks[31], (DEPTH, FF_DIM, D), FF_DIM ** -0.5),
    }


def reference(x_prompt, x_sample, state_pool, state_ssm_re, state_ssm_im, state_shortconv, state_conformer,
              norm1_g, norm2_g, final_g, w_in, b_gate, pool_w, pool_scale, lam_re, lam_im, log_dt,
              b_re, b_im, c_re, c_im, d_skip, w_glu, b_glu, sc_w, cf_w, cf_ln_g, cf_ln_b,
              w_branch, w_out, w_ffn_in, w_ffn_out):
    weights = (norm1_g, norm2_g, final_g, w_in, b_gate, pool_w, pool_scale, lam_re, lam_im, log_dt,
               b_re, b_im, c_re, c_im, d_skip, w_glu, b_glu, sc_w, cf_w, cf_ln_g, cf_ln_b,
               w_branch, w_out, w_ffn_in, w_ffn_out)
    nb = x_prompt.shape[0]
    sd = state_pool.dtype
    z_pool = jnp.zeros((DEPTH, nb, POOL_BUF, MIX_W), sd)
    z_ssm = jnp.zeros((DEPTH, nb, SSM_GROUPS, SSM_STATE), state_ssm_re.dtype)
    z_sc = jnp.zeros((DEPTH, nb, SC_WIDTH - 1, MIX_W), state_shortconv.dtype)
    z_cf = jnp.zeros((DEPTH, nb, CF_WIDTH - 1, MIX_W), state_conformer.dtype)
    y_prompt, pool_p, ssm_re_p, ssm_im_p, sc_p, cf_p = trunk(x_prompt, 0, z_pool, z_ssm, z_ssm, z_sc, z_cf, weights)
    y_sample, pool_s, ssm_re_s, ssm_im_s, sc_s, cf_s = trunk(x_sample, PAST_LEN, state_pool, state_ssm_re, state_ssm_im,
                                                             state_shortconv, state_conformer, weights)
    return (y_prompt, y_sample, pool_p, pool_s, ssm_re_p, ssm_re_s, ssm_im_p, ssm_im_s, sc_p, sc_s, cf_p, cf_s)
```

```python
import functools
import math

import jax
import jax.numpy as jnp
from jax import lax
from jax.experimental import pallas as pl
from jax.experimental.pallas import tpu as pltpu

D_MODEL = 1024
DEPTH = 4
PAST_LEN = 16384
N_BRANCH = 4
MIX_W = D_MODEL // 4
POOL_WINDOWS = (2, 4, 8, 16)
POOL_GW = MIX_W // len(POOL_WINDOWS)
POOL_BUF = max(POOL_WINDOWS) - 1
SSM_GROUP_CH = 16
SSM_GROUPS = MIX_W // SSM_GROUP_CH
SSM_STATE = 64
SSM_W = SSM_GROUPS * SSM_STATE
SC_WIDTH = 3
CF_WIDTH = 31
FF_DIM = -(-8 * D_MODEL // (3 * 256)) * 256
OFF_POOL = 0
OFF_SSM = OFF_POOL + MIX_W
OFF_SC = OFF_SSM + MIX_W
OFF_CF = OFF_SC + 3 * MIX_W
OFF_GATE = OFF_CF + 2 * MIX_W
IN_COLS = OFF_GATE + N_BRANCH * D_MODEL
RMS_EPS = 1e-6
LN_EPS = 1e-5

SUBLANES = 8
VMEM_LIMIT_BYTES = 56 * 1024 * 1024
FFN_COL_CHUNKS = ((0, 1024), (1024, 2048), (2048, FF_DIM))

f32 = jnp.float32
bf16 = jnp.bfloat16


def _sigmoid(x):
    return 0.5 * jnp.tanh(0.5 * x) + 0.5


def _gelu_tanh(x):
    c = math.sqrt(2.0 / math.pi)
    return x * (0.5 * (1.0 + jnp.tanh(c * (x + 0.044715 * (x * x * x)))))


def _rmsnorm(x, g):
    y = x * lax.rsqrt(jnp.mean(x * x, axis=-1, keepdims=True) + RMS_EPS)
    return y * g


def _dot(a, b):
    return jnp.dot(a, b, preferred_element_type=f32)


def _mixer_kernel(x_ref, sp_ref, sre_ref, sim_ref, ssc_ref, scf_ref,
                  n1_ref, win_ref, bgate_ref, wpool_ref, pscale_ref, lam_ref, bmat_ref, cmat_ref,
                  dskip_ref, wglu_ref, bglu_ref, scw_ref, cfw_ref, lng_ref, lnb_ref, wbr_ref, wout_ref,
                  xo_ref, po_ref, reo_ref, imo_ref, sco_ref, cfo_ref,
                  proj_s, bu_s, extp_s, exts_s, extc_s, h_s,
                  *, tt, nsb, pos0):
    rows = tt * nsb
    hp, hs, hc = POOL_BUF * nsb, (SC_WIDTH - 1) * nsb, (CF_WIDTH - 1) * nsb
    t_idx = pl.program_id(1)

    @pl.when(t_idx == 0)
    def _load_state():
        extp_s[0:hp, :] = sp_ref[...].reshape(hp, MIX_W)
        exts_s[0:hs, :] = ssc_ref[...].reshape(hs, MIX_W)
        extc_s[0:hc, :] = scf_ref[...].reshape(hc, MIX_W)
        h_s[:, 0:SSM_W] = sre_ref[...]
        h_s[:, SSM_W:] = sim_ref[...]

    x = x_ref[...].reshape(rows, D_MODEL)
    hn = _rmsnorm(x, n1_ref[...]).astype(bf16)
    proj_s[...] = _dot(hn, win_ref[:, 0:OFF_GATE])

    v = proj_s[:, OFF_POOL:OFF_POOL + MIX_W]
    extp_s[hp:hp + rows, :] = v
    ext = extp_s[...]
    s2 = ext[nsb:, :] + ext[:-nsb, :]
    s4 = s2[2 * nsb:, :] + s2[:-2 * nsb, :]
    s8 = s4[4 * nsb:, :] + s4[:-4 * nsb, :]
    s16 = s8[8 * nsb:, :] + s8[:-8 * nsb, :]
    lane = lax.broadcasted_iota(jnp.int32, (1, MIX_W), 1)
    grp = lane // POOL_GW
    wsum = jnp.where(grp == 0, s2[14 * nsb:, :],
                     jnp.where(grp == 1, s4[12 * nsb:, :],
                               jnp.where(grp == 2, s8[8 * nsb:, :], s16)))
    win = jnp.where(grp == 0, 2.0, jnp.where(grp == 1, 4.0, jnp.where(grp == 2, 8.0, 16.0))).astype(f32)
    step = (lax.broadcasted_iota(jnp.int32, (rows, MIX_W), 0) >> (nsb.bit_length() - 1)) + t_idx * tt
    pos = step.astype(f32) + float(pos0)
    cnt = jnp.minimum(pos + 1.0, win)
    d = wsum / cnt - v
    ya = _dot(d.astype(bf16), wpool_ref[...]) * pscale_ref[...]
    new_pool = extp_s[rows:rows + hp, :]
    extp_s[0:hp, :] = new_pool
    po_ref[...] = new_pool.reshape(POOL_BUF, nsb, MIX_W)

    u = proj_s[:, OFF_SSM:OFF_SSM + MIX_W]
    bu_s[...] = _dot(u.astype(bf16), bmat_ref[...])
    lam_r = jnp.broadcast_to(lam_ref[0:1, :], (SUBLANES, SSM_W))
    lam_i = jnp.broadcast_to(lam_ref[1:2, :], (SUBLANES, SSM_W))
    for j in range(nsb // SUBLANES):
        r0 = j * SUBLANES

        def _scan_step(t, carry, r0=r0):
            hr, hi = carry
            row = pl.multiple_of(t * nsb + r0, SUBLANES)
            nr = lam_r * hr - lam_i * hi + bu_s[pl.ds(row, SUBLANES), 0:SSM_W]
            ni = lam_r * hi + lam_i * hr + bu_s[pl.ds(row, SUBLANES), SSM_W:]
            bu_s[pl.ds(row, SUBLANES), 0:SSM_W] = nr
            bu_s[pl.ds(row, SUBLANES), SSM_W:] = ni
            return nr, ni

        hr, hi = lax.fori_loop(0, tt, _scan_step,
                               (h_s[r0:r0 + SUBLANES, 0:SSM_W], h_s[r0:r0 + SUBLANES, SSM_W:]),
                               unroll=min(tt, 8))
        h_s[r0:r0 + SUBLANES, 0:SSM_W] = hr
        h_s[r0:r0 + SUBLANES, SSM_W:] = hi
    reo_ref[...] = h_s[:, 0:SSM_W]
    imo_ref[...] = h_s[:, SSM_W:]
    y = _dot(bu_s[...].astype(bf16), cmat_ref[...]) + dskip_ref[...] * u
    z = _gelu_tanh(y)
    yb = z * _sigmoid(_dot(z.astype(bf16), wglu_ref[...]) + bglu_ref[...])

    bg = proj_s[:, OFF_SC:OFF_SC + MIX_W]
    cg = proj_s[:, OFF_SC + MIX_W:OFF_SC + 2 * MIX_W]
    hx = proj_s[:, OFF_SC + 2 * MIX_W:OFF_CF]
    exts_s[hs:hs + rows, :] = cg * hx
    conv = scw_ref[0:1, :] * exts_s[0:rows, :]
    for k in range(1, SC_WIDTH):
        conv = conv + scw_ref[k:k + 1, :] * exts_s[k * nsb:k * nsb + rows, :]
    yc = bg * conv
    new_sc = exts_s[rows:rows + hs, :]
    exts_s[0:hs, :] = new_sc
    sco_ref[...] = new_sc.reshape(SC_WIDTH - 1, nsb, MIX_W)

    ga = proj_s[:, OFF_CF:OFF_CF + MIX_W]
    gb = proj_s[:, OFF_CF + MIX_W:OFF_GATE]
    extc_s[hc:hc + rows, :] = ga * _sigmoid(gb)
    conv = cfw_ref[0:1, :] * extc_s[0:rows, :]
    for k in range(1, CF_WIDTH):
        conv = conv + cfw_ref[k:k + 1, :] * extc_s[k * nsb:k * nsb + rows, :]
    mu = jnp.mean(conv, axis=-1, keepdims=True)
    cen = conv - mu
    var = jnp.mean(cen * cen, axis=-1, keepdims=True)
    yn = cen * lax.rsqrt(var + LN_EPS) * lng_ref[...] + lnb_ref[...]
    yd = yn * _sigmoid(yn)
    new_cf = extc_s[rows:rows + hc, :]
    extc_s[0:hc, :] = new_cf
    cfo_ref[...] = new_cf.reshape(CF_WIDTH - 1, nsb, MIX_W)

    merged = None
    for k, br in enumerate((ya, yb, yc, yd)):
        lo = OFF_GATE + k * D_MODEL
        gate = _sigmoid(_dot(hn, win_ref[:, lo:lo + D_MODEL]) + bgate_ref[k:k + 1, :])
        term = gate * _dot(br.astype(bf16), wbr_ref[k])
        merged = term if merged is None else merged + term
    xo = x + _dot(merged.astype(bf16), wout_ref[...])
    xo_ref[...] = xo.reshape(tt, nsb, D_MODEL)


def _const_spec(block_shape, index):
    return pl.BlockSpec(block_shape, lambda s, t: index, pipeline_mode=pl.Buffered(1))


def _mixer_call(layer, x3, st, wts, *, tt, nsb, pos0):
    t_total, n_seq, _ = x3.shape
    assert nsb % SUBLANES == 0 and nsb & (nsb - 1) == 0 and n_seq % nsb == 0 and t_total % tt == 0
    rows = tt * nsb
    grid = (n_seq // nsb, t_total // tt)
    sp, sre, sim, ssc, scf = st
    c = MIX_W

    def hist_spec(h):
        return pl.BlockSpec((None, h, nsb, c), lambda s, t: (layer, 0, s, 0))

    ssm_spec = pl.BlockSpec((None, nsb, SSM_W), lambda s, t: (layer, s, 0))
    x_spec = pl.BlockSpec((tt, nsb, D_MODEL), lambda s, t: (t, s, 0))
    l3 = (layer, 0, 0)
    in_specs = [
        x_spec, hist_spec(POOL_BUF), ssm_spec, ssm_spec, hist_spec(SC_WIDTH - 1), hist_spec(CF_WIDTH - 1),
        _const_spec((None, 1, D_MODEL), l3),
        _const_spec((None, D_MODEL, IN_COLS), l3),
        _const_spec((None, N_BRANCH, D_MODEL), l3),
        _const_spec((None, c, c), l3),
        _const_spec((None, 1, c), l3),
        _const_spec((None, 2, SSM_W), l3),
        _const_spec((None, c, 2 * SSM_W), l3),
        _const_spec((None, 2 * SSM_W, c), l3),
        _const_spec((None, 1, c), l3),
        _const_spec((None, c, c), l3),
        _const_spec((None, 1, c), l3),
        _const_spec((None, SC_WIDTH, c), l3),
        _const_spec((None, CF_WIDTH, c), l3),
        _const_spec((None, 1, c), l3),
        _const_spec((None, 1, c), l3),
        _const_spec((None, N_BRANCH, c, D_MODEL), (layer, 0, 0, 0)),
        _const_spec((None, D_MODEL, D_MODEL), l3),
    ]

    def hist_out(h):
        return pl.BlockSpec((h, nsb, c), lambda s, t: (0, s, 0))

    ssm_out = pl.BlockSpec((nsb, SSM_W), lambda s, t: (s, 0))
    out_specs = [x_spec, hist_out(POOL_BUF), ssm_out, ssm_out, hist_out(SC_WIDTH - 1), hist_out(CF_WIDTH - 1)]
    out_shape = [
        jax.ShapeDtypeStruct(x3.shape, f32),
        jax.ShapeDtypeStruct((POOL_BUF, n_seq, c), f32),
        jax.ShapeDtypeStruct((n_seq, SSM_W), f32),
        jax.ShapeDtypeStruct((n_seq, SSM_W), f32),
        jax.ShapeDtypeStruct((SC_WIDTH - 1, n_seq, c), f32),
        jax.ShapeDtypeStruct((CF_WIDTH - 1, n_seq, c), f32),
    ]
    scratch = [
        pltpu.VMEM((rows, OFF_GATE), f32),
        pltpu.VMEM((rows, 2 * SSM_W), f32),
        pltpu.VMEM((POOL_BUF * nsb + rows, c), f32),
        pltpu.VMEM(((SC_WIDTH - 1) * nsb + rows, c), f32),
        pltpu.VMEM(((CF_WIDTH - 1) * nsb + rows, c), f32),
        pltpu.VMEM((nsb, 2 * SSM_W), f32),
    ]
    kern = functools.partial(_mixer_kernel, tt=tt, nsb=nsb, pos0=pos0)
    return pl.pallas_call(
        kern, grid=grid, in_specs=in_specs, out_specs=out_specs, out_shape=out_shape,
        scratch_shapes=scratch,
        compiler_params=pltpu.CompilerParams(dimension_semantics=("arbitrary", "arbitrary"),
                                             vmem_limit_bytes=VMEM_LIMIT_BYTES),
        name=f"mixer_l{layer}_n{nsb}",
    )(x3, sp, sre, sim, ssc, scf, *wts)


def _ffn_kernel(x_ref, n2_ref, wfi_ref, wfo_ref, fg_ref, o_ref, *, final):
    x = x_ref[...]
    hn = _rmsnorm(x, n2_ref[...]).astype(bf16)
    acc = x
    for lo, hi in FFN_COL_CHUNKS:
        g = _dot(hn, wfi_ref[:, lo:hi])
        up = _dot(hn, wfi_ref[:, FF_DIM + lo:FF_DIM + hi])
        act = (g * _sigmoid(g)) * up
        acc = acc + _dot(act.astype(bf16), wfo_ref[lo:hi, :])
    if final:
        acc = _rmsnorm(acc, fg_ref[...])
    o_ref[...] = acc


def _ffn_call(layer, x2, n2, wfi, wfo, fg, *, rows, final):
    total = x2.shape[0]
    x_spec = pl.BlockSpec((rows, D_MODEL), lambda r: (r, 0))

    def cspec(block_shape, index):
        return pl.BlockSpec(block_shape, lambda r: index, pipeline_mode=pl.Buffered(1))

    return pl.pallas_call(
        functools.partial(_ffn_kernel, final=final),
        grid=(total // rows,),
        in_specs=[x_spec,
                  cspec((None, 1, D_MODEL), (layer, 0, 0)),
                  cspec((None, D_MODEL, 2 * FF_DIM), (layer, 0, 0)),
                  cspec((None, FF_DIM, D_MODEL), (layer, 0, 0)),
                  cspec((1, D_MODEL), (0, 0))],
        out_specs=x_spec,
        out_shape=jax.ShapeDtypeStruct(x2.shape, f32),
        compiler_params=pltpu.CompilerParams(dimension_semantics=("arbitrary",),
                                             vmem_limit_bytes=VMEM_LIMIT_BYTES),
        name=f"ffn_l{layer}_r{total}",
    )(x2, n2, wfi, wfo, fg)


def _ssm_params(lam_re, lam_im, log_dt, b_re, b_im, c_re, c_im):
    g, p, gc = SSM_GROUPS, SSM_STATE, SSM_GROUP_CH
    dt = jnp.exp(log_dt)[..., None]
    mag = jnp.exp(lam_re * dt)
    lbr, lbi = mag * jnp.cos(lam_im * dt), mag * jnp.sin(lam_im * dt)
    den = lam_re * lam_re + lam_im * lam_im
    qr = ((lbr - 1.0) * lam_re + lbi * lam_im) / den
    qi = (lbi * lam_re - (lbr - 1.0) * lam_im) / den
    bbr = qr[..., None] * b_re - qi[..., None] * b_im
    bbi = qr[..., None] * b_im + qi[..., None] * b_re
    eye = jnp.eye(g, dtype=f32)

    def to_bmat(b):
        bt = jnp.transpose(b, (0, 1, 3, 2))
        return (bt[:, :, :, None, :] * eye[None, :, None, :, None]).reshape(DEPTH, g * gc, g * p)

    def to_cmat(cm):
        ct = jnp.transpose(cm, (0, 1, 3, 2))
        return (ct[:, :, :, None, :] * eye[None, :, None, :, None]).reshape(DEPTH, g * p, g * gc)

    bmat = jnp.concatenate([to_bmat(bbr), to_bmat(bbi)], axis=-1).astype(bf16)
    cmat = jnp.concatenate([to_cmat(c_re), to_cmat(-c_im)], axis=-2).astype(bf16)
    lam = jnp.stack([lbr.reshape(DEPTH, g * p), lbi.reshape(DEPTH, g * p)], axis=1)
    return lam, bmat, cmat


def _pool_blockdiag(pool_w):
    k = len(POOL_WINDOWS)
    eye = jnp.eye(k, dtype=f32)
    w = pool_w[:, :, :, None, :] * eye[None, :, None, :, None]
    return w.reshape(DEPTH, MIX_W, MIX_W).astype(bf16)


def _time_major_state(s):
    return jnp.transpose(s, (0, 2, 1, 3))


def kernel(x_prompt, x_sample, state_pool, state_ssm_re, state_ssm_im, state_shortconv, state_conformer,
           norm1_g, norm2_g, final_g, w_in, b_gate, pool_w, pool_scale, lam_re, lam_im, log_dt,
           b_re, b_im, c_re, c_im, d_skip, w_glu, b_glu, sc_w, cf_w, cf_ln_g, cf_ln_b,
           w_branch, w_out, w_ffn_in, w_ffn_out):
    nb = x_prompt.shape[0]
    lam, bmat, cmat = _ssm_params(lam_re, lam_im, log_dt, b_re, b_im, c_re, c_im)
    row = lambda a: a.reshape(DEPTH, 1, a.shape[-1])
    mixer_w = (row(norm1_g), w_in.astype(bf16), b_gate, _pool_blockdiag(pool_w), row(pool_scale),
               lam, bmat, cmat, row(d_skip), w_glu.astype(bf16), row(b_glu), sc_w, cf_w,
               row(cf_ln_g), row(cf_ln_b), w_branch.astype(bf16), w_out.astype(bf16))
    n2 = row(norm2_g)
    wfi, wfo = w_ffn_in.astype(bf16), w_ffn_out.astype(bf16)
    fg = final_g.reshape(1, D_MODEL)

    groups = []
    zeros = lambda h: jnp.zeros((DEPTH, h, nb, MIX_W), f32)
    z_ssm = jnp.zeros((DEPTH, nb, SSM_W), f32)
    groups.append(dict(x=jnp.transpose(x_prompt, (1, 0, 2)), pos0=0, tt=64, nsb=nb,
                       st=(zeros(POOL_BUF), z_ssm, z_ssm, zeros(SC_WIDTH - 1), zeros(CF_WIDTH - 1))))
    ns = x_sample.shape[0]
    groups.append(dict(x=jnp.transpose(x_sample, (1, 0, 2)), pos0=PAST_LEN, tt=x_sample.shape[1], nsb=64,
                       st=(_time_major_state(state_pool), state_ssm_re.reshape(DEPTH, ns, SSM_W),
                           state_ssm_im.reshape(DEPTH, ns, SSM_W), _time_major_state(state_shortconv),
                           _time_major_state(state_conformer))))

    results = []
    for grp in groups:
        x3 = grp["x"]
        t_total, n_seq, _ = x3.shape
        outs = [[] for _ in range(5)]
        for layer in range(DEPTH):
            x3, po, reo, imo, sco, cfo = _mixer_call(layer, x3, grp["st"], mixer_w,
                                                     tt=grp["tt"], nsb=grp["nsb"], pos0=grp["pos0"])
            for lst, o in zip(outs, (po, reo, imo, sco, cfo)):
                lst.append(o)
            x2 = _ffn_call(layer, x3.reshape(t_total * n_seq, D_MODEL), n2, wfi, wfo, fg,
                           rows=512, final=(layer == DEPTH - 1))
            x3 = x2.reshape(t_total, n_seq, D_MODEL)
        y = jnp.transpose(x3, (1, 0, 2))
        pool = jnp.transpose(jnp.stack(outs[0]), (0, 2, 1, 3))
        sre = jnp.stack(outs[1]).reshape(DEPTH, n_seq, SSM_GROUPS, SSM_STATE)
        sim = jnp.stack(outs[2]).reshape(DEPTH, n_seq, SSM_GROUPS, SSM_STATE)
        sc = jnp.transpose(jnp.stack(outs[3]), (0, 2, 1, 3))
        cf = jnp.transpose(jnp.stack(outs[4]), (0, 2, 1, 3))
        results.append((y, pool, sre, sim, sc, cf))

    (yp, pp, rp, ip, cp, fp), (ys, ps, rs, is_, cs, fs) = results
    return (yp, ys, pp, ps, rp, rs, ip, is_, cp, cs, fp, fs)
```

```python
import functools
import math

import jax
import jax.numpy as jnp
from jax import lax
from jax.experimental import pallas as pl
from jax.experimental.pallas import tpu as pltpu

D_MODEL = 1024
DEPTH = 4
PAST_LEN = 16384
N_BRANCH = 4
MIX_W = D_MODEL // 4
POOL_WINDOWS = (2, 4, 8, 16)
POOL_GW = MIX_W // len(POOL_WINDOWS)
POOL_BUF = max(POOL_WINDOWS) - 1
SSM_GROUP_CH = 16
SSM_GROUPS = MIX_W // SSM_GROUP_CH
SSM_STATE = 64
SSM_W = SSM_GROUPS * SSM_STATE
SC_WIDTH = 3
CF_WIDTH = 31
FF_DIM = -(-8 * D_MODEL // (3 * 256)) * 256
OFF_POOL = 0
OFF_SSM = OFF_POOL + MIX_W
OFF_SC = OFF_SSM + MIX_W
OFF_CF = OFF_SC + 3 * MIX_W
OFF_GATE = OFF_CF + 2 * MIX_W
IN_COLS = OFF_GATE + N_BRANCH * D_MODEL
RMS_EPS = 1e-6
LN_EPS = 1e-5

SUBLANES = 8
VMEM_LIMIT_BYTES = 56 * 1024 * 1024
FFN_COL_CHUNKS = ((0, 1024), (1024, 2048), (2048, FF_DIM))
CONV_ROW_CHUNK = 64

f32 = jnp.float32
bf16 = jnp.bfloat16


def _sigmoid(x):
    return 0.5 * jnp.tanh(0.5 * x) + 0.5


def _gelu_tanh(x):
    c = math.sqrt(2.0 / math.pi)
    return x * (0.5 * (1.0 + jnp.tanh(c * (x + 0.044715 * (x * x * x)))))


def _rmsnorm(x, g):
    y = x * lax.rsqrt(jnp.mean(x * x, axis=-1, keepdims=True) + RMS_EPS)
    return y * g


def _dot(a, b):
    return jnp.dot(a, b, preferred_element_type=f32)


def _dependent_zero(v):
    bits = lax.bitcast_convert_type(v, jnp.uint32)
    return lax.bitcast_convert_type((bits >> 16) >> 16, f32)


def _causal_conv(ext_s, w_ref, width, rows, nsb):
    blocks = CONV_ROW_CHUNK // SUBLANES
    outs = []
    w0 = w_ref[0]
    for c0 in range(0, rows, CONV_ROW_CHUNK):
        acc = None
        for k in range(width):
            r0 = k * nsb + c0
            wk = w0 if k == 0 else w_ref[k]
            tap = wk[None] * ext_s[r0:r0 + CONV_ROW_CHUNK, :].reshape(blocks, SUBLANES, MIX_W)
            acc = tap if acc is None else acc + tap
        outs.append(acc.reshape(CONV_ROW_CHUNK, MIX_W))
        w0 = w_ref[0] + _dependent_zero(acc[blocks - 1])
    return jnp.concatenate(outs, axis=0)


def _mixer_kernel(x_ref, sp_ref, sre_ref, sim_ref, ssc_ref, scf_ref,
                  n1_ref, win_ref, bgate_ref, wpool_ref, pscale_ref, lam_ref, bmat_ref, cmat_ref,
                  dskip_ref, wglu_ref, bglu_ref, scw_ref, cfw_ref, lng_ref, lnb_ref, wbr_ref, wout_ref,
                  xo_ref, po_ref, reo_ref, imo_ref, sco_ref, cfo_ref,
                  bu_s, extp_s, exts_s, extc_s, h_s,
                  *, tt, nsb, pos0):
    rows = tt * nsb
    hp, hs, hc = POOL_BUF * nsb, (SC_WIDTH - 1) * nsb, (CF_WIDTH - 1) * nsb
    t_idx = pl.program_id(1)

    @pl.when(t_idx == 0)
    def _load_state():
        extp_s[0:hp, :] = sp_ref[...].reshape(hp, MIX_W)
        exts_s[0:hs, :] = ssc_ref[...].reshape(hs, MIX_W)
        extc_s[0:hc, :] = scf_ref[...].reshape(hc, MIX_W)
        h_s[:, 0:SSM_W] = sre_ref[...]
        h_s[:, SSM_W:] = sim_ref[...]

    x = x_ref[...].reshape(rows, D_MODEL)
    hn = _rmsnorm(x, n1_ref[...]).astype(bf16)

    def proj(lo, hi):
        return _dot(hn, win_ref[:, lo:hi])

    def gate(k):
        lo = OFF_GATE + k * D_MODEL
        return _sigmoid(proj(lo, lo + D_MODEL) + bgate_ref[k:k + 1, :])

    vu = proj(OFF_POOL, OFF_SC)
    v, u = vu[:, 0:MIX_W], vu[:, MIX_W:]
    ab = proj(OFF_CF, OFF_GATE)
    bu_s[...] = _dot(u.astype(bf16), bmat_ref[...])
    scp = proj(OFF_SC, OFF_CF)
    gate_c = gate(2)
    gate_a = gate(0)

    extc_s[hc:hc + rows, :] = ab[:, 0:MIX_W] * _sigmoid(ab[:, MIX_W:])
    conv = _causal_conv(extc_s, cfw_ref, CF_WIDTH, rows, nsb)
    new_cf = extc_s[rows:rows + hc, :]
    extc_s[0:hc, :] = new_cf
    cfo_ref[...] = new_cf.reshape(CF_WIDTH - 1, nsb, MIX_W)

    exts_s[hs:hs + rows, :] = scp[:, MIX_W:2 * MIX_W] * scp[:, 2 * MIX_W:]
    yc = scp[:, 0:MIX_W] * _causal_conv(exts_s, scw_ref, SC_WIDTH, rows, nsb)
    new_sc = exts_s[rows:rows + hs, :]
    exts_s[0:hs, :] = new_sc
    sco_ref[...] = new_sc.reshape(SC_WIDTH - 1, nsb, MIX_W)

    extp_s[hp:hp + rows, :] = v
    ext = extp_s[...]
    s2 = ext[nsb:, :] + ext[:-nsb, :]
    s4 = s2[2 * nsb:, :] + s2[:-2 * nsb, :]
    s8 = s4[4 * nsb:, :] + s4[:-4 * nsb, :]
    s16 = s8[8 * nsb:, :] + s8[:-8 * nsb, :]
    lane = lax.broadcasted_iota(jnp.int32, (1, MIX_W), 1)
    grp = lane // POOL_GW
    wsum = jnp.where(grp == 0, s2[14 * nsb:, :],
                     jnp.where(grp == 1, s4[12 * nsb:, :],
                               jnp.where(grp == 2, s8[8 * nsb:, :], s16)))
    win = jnp.where(grp == 0, 2.0, jnp.where(grp == 1, 4.0, jnp.where(grp == 2, 8.0, 16.0))).astype(f32)
    step = (lax.broadcasted_iota(jnp.int32, (rows, MIX_W), 0) >> (nsb.bit_length() - 1)) + t_idx * tt
    pos = step.astype(f32) + float(pos0)
    cnt = jnp.minimum(pos + 1.0, win)
    d = wsum / cnt - v
    ya = _dot(d.astype(bf16), wpool_ref[...]) * pscale_ref[...]
    new_pool = extp_s[rows:rows + hp, :]
    extp_s[0:hp, :] = new_pool
    po_ref[...] = new_pool.reshape(POOL_BUF, nsb, MIX_W)

    merged = gate_a * _dot(ya.astype(bf16), wbr_ref[0]) + gate_c * _dot(yc.astype(bf16), wbr_ref[2])
    gate_d = gate(3)

    lam_r = jnp.broadcast_to(lam_ref[0:1, :], (SUBLANES, SSM_W))
    lam_i = jnp.broadcast_to(lam_ref[1:2, :], (SUBLANES, SSM_W))
    for j in range(nsb // SUBLANES):
        r0 = j * SUBLANES
        hr, hi = h_s[r0:r0 + SUBLANES, 0:SSM_W], h_s[r0:r0 + SUBLANES, SSM_W:]
        for t in range(tt):
            row = t * nsb + r0
            nr = lam_r * hr - lam_i * hi + bu_s[row:row + SUBLANES, 0:SSM_W]
            ni = lam_r * hi + lam_i * hr + bu_s[row:row + SUBLANES, SSM_W:]
            bu_s[row:row + SUBLANES, 0:SSM_W] = nr
            bu_s[row:row + SUBLANES, SSM_W:] = ni
            hr, hi = nr, ni
        h_s[r0:r0 + SUBLANES, 0:SSM_W] = hr
        h_s[r0:r0 + SUBLANES, SSM_W:] = hi
    reo_ref[...] = h_s[:, 0:SSM_W]
    imo_ref[...] = h_s[:, SSM_W:]
    y = (_dot(bu_s[:, 0:SSM_W].astype(bf16), cmat_ref[0:SSM_W, :])
         + _dot(bu_s[:, SSM_W:].astype(bf16), cmat_ref[SSM_W:, :]) + dskip_ref[...] * u)

    mu = jnp.mean(conv, axis=-1, keepdims=True)
    cen = conv - mu
    var = jnp.mean(cen * cen, axis=-1, keepdims=True)
    yn = cen * lax.rsqrt(var + LN_EPS) * lng_ref[...] + lnb_ref[...]
    yd = yn * _sigmoid(yn)
    merged = merged + gate_d * _dot(yd.astype(bf16), wbr_ref[3])
    gate_b = gate(1)

    z = _gelu_tanh(y)
    yb = z * _sigmoid(_dot(z.astype(bf16), wglu_ref[...]) + bglu_ref[...])
    merged = merged + gate_b * _dot(yb.astype(bf16), wbr_ref[1])

    xo = x + _dot(merged.astype(bf16), wout_ref[...])
    xo_ref[...] = xo.reshape(tt, nsb, D_MODEL)


def _const_spec(block_shape, index):
    return pl.BlockSpec(block_shape, lambda s, t: index, pipeline_mode=pl.Buffered(1))


def _mixer_call(layer, x3, st, wts, *, tt, nsb, pos0):
    t_total, n_seq, _ = x3.shape
    assert nsb % SUBLANES == 0 and nsb & (nsb - 1) == 0 and n_seq % nsb == 0 and t_total % tt == 0
    rows = tt * nsb
    grid = (n_seq // nsb, t_total // tt)
    sp, sre, sim, ssc, scf = st
    c = MIX_W

    def hist_spec(h):
        return pl.BlockSpec((None, h, nsb, c), lambda s, t: (layer, 0, s, 0))

    ssm_spec = pl.BlockSpec((None, nsb, SSM_W), lambda s, t: (layer, s, 0))
    x_spec = pl.BlockSpec((tt, nsb, D_MODEL), lambda s, t: (t, s, 0))
    l3 = (layer, 0, 0)
    in_specs = [
        x_spec, hist_spec(POOL_BUF), ssm_spec, ssm_spec, hist_spec(SC_WIDTH - 1), hist_spec(CF_WIDTH - 1),
        _const_spec((None, 1, D_MODEL), l3),
        _const_spec((None, D_MODEL, IN_COLS), l3),
        _const_spec((None, N_BRANCH, D_MODEL), l3),
        _const_spec((None, c, c), l3),
        _const_spec((None, 1, c), l3),
        _const_spec((None, 2, SSM_W), l3),
        _const_spec((None, c, 2 * SSM_W), l3),
        _const_spec((None, 2 * SSM_W, c), l3),
        _const_spec((None, 1, c), l3),
        _const_spec((None, c, c), l3),
        _const_spec((None, 1, c), l3),
        _const_spec((None, SC_WIDTH, SUBLANES, c), (layer, 0, 0, 0)),
        _const_spec((None, CF_WIDTH, SUBLANES, c), (layer, 0, 0, 0)),
        _const_spec((None, 1, c), l3),
        _const_spec((None, 1, c), l3),
        _const_spec((None, N_BRANCH, c, D_MODEL), (layer, 0, 0, 0)),
        _const_spec((None, D_MODEL, D_MODEL), l3),
    ]

    def hist_out(h):
        return pl.BlockSpec((h, nsb, c), lambda s, t: (0, s, 0))

    ssm_out = pl.BlockSpec((nsb, SSM_W), lambda s, t: (s, 0))
    out_specs = [x_spec, hist_out(POOL_BUF), ssm_out, ssm_out, hist_out(SC_WIDTH - 1), hist_out(CF_WIDTH - 1)]
    out_shape = [
        jax.ShapeDtypeStruct(x3.shape, f32),
        jax.ShapeDtypeStruct((POOL_BUF, n_seq, c), f32),
        jax.ShapeDtypeStruct((n_seq, SSM_W), f32),
        jax.ShapeDtypeStruct((n_seq, SSM_W), f32),
        jax.ShapeDtypeStruct((SC_WIDTH - 1, n_seq, c), f32),
        jax.ShapeDtypeStruct((CF_WIDTH - 1, n_seq, c), f32),
    ]
    scratch = [
        pltpu.VMEM((rows, 2 * SSM_W), f32),
        pltpu.VMEM((POOL_BUF * nsb + rows, c), f32),
        pltpu.VMEM(((SC_WIDTH - 1) * nsb + rows, c), f32),
        pltpu.VMEM(((CF_WIDTH - 1) * nsb + rows, c), f32),
        pltpu.VMEM((nsb, 2 * SSM_W), f32),
    ]
    kern = functools.partial(_mixer_kernel, tt=tt, nsb=nsb, pos0=pos0)
    return pl.pallas_call(
        kern, grid=grid, in_specs=in_specs, out_specs=out_specs, out_shape=out_shape,
        scratch_shapes=scratch,
        compiler_params=pltpu.CompilerParams(dimension_semantics=("arbitrary", "arbitrary"),
                                             vmem_limit_bytes=VMEM_LIMIT_BYTES),
        name=f"mixer_l{layer}_n{nsb}",
    )(x3, sp, sre, sim, ssc, scf, *wts)


def _ffn_kernel(x_ref, n2_ref, wfi_ref, wfo_ref, fg_ref, o_ref, *, final):
    x = x_ref[...]
    hn = _rmsnorm(x, n2_ref[...]).astype(bf16)
    acc = x
    for lo, hi in FFN_COL_CHUNKS:
        g = _dot(hn, wfi_ref[:, lo:hi])
        up = _dot(hn, wfi_ref[:, FF_DIM + lo:FF_DIM + hi])
        act = (g * _sigmoid(g)) * up
        acc = acc + _dot(act.astype(bf16), wfo_ref[lo:hi, :])
    if final:
        acc = _rmsnorm(acc, fg_ref[...])
    o_ref[...] = acc


def _ffn_call(layer, x2, n2, wfi, wfo, fg, *, rows, final):
    total = x2.shape[0]
    x_spec = pl.BlockSpec((rows, D_MODEL), lambda r: (r, 0))

    def cspec(block_shape, index):
        return pl.BlockSpec(block_shape, lambda r: index, pipeline_mode=pl.Buffered(1))

    return pl.pallas_call(
        functools.partial(_ffn_kernel, final=final),
        grid=(total // rows,),
        in_specs=[x_spec,
                  cspec((None, 1, D_MODEL), (layer, 0, 0)),
                  cspec((None, D_MODEL, 2 * FF_DIM), (layer, 0, 0)),
                  cspec((None, FF_DIM, D_MODEL), (layer, 0, 0)),
                  cspec((1, D_MODEL), (0, 0))],
        out_specs=x_spec,
        out_shape=jax.ShapeDtypeStruct(x2.shape, f32),
        compiler_params=pltpu.CompilerParams(dimension_semantics=("arbitrary",),
                                             vmem_limit_bytes=VMEM_LIMIT_BYTES),
        name=f"ffn_l{layer}_r{total}",
    )(x2, n2, wfi, wfo, fg)


def _ssm_params(lam_re, lam_im, log_dt, b_re, b_im, c_re, c_im):
    g, p, gc = SSM_GROUPS, SSM_STATE, SSM_GROUP_CH
    dt = jnp.exp(log_dt)[..., None]
    mag = jnp.exp(lam_re * dt)
    lbr, lbi = mag * jnp.cos(lam_im * dt), mag * jnp.sin(lam_im * dt)
    den = lam_re * lam_re + lam_im * lam_im
    qr = ((lbr - 1.0) * lam_re + lbi * lam_im) / den
    qi = (lbi * lam_re - (lbr - 1.0) * lam_im) / den
    bbr = qr[..., None] * b_re - qi[..., None] * b_im
    bbi = qr[..., None] * b_im + qi[..., None] * b_re
    eye = jnp.eye(g, dtype=f32)

    def to_bmat(b):
        bt = jnp.transpose(b, (0, 1, 3, 2))
        return (bt[:, :, :, None, :] * eye[None, :, None, :, None]).reshape(DEPTH, g * gc, g * p)

    def to_cmat(cm):
        ct = jnp.transpose(cm, (0, 1, 3, 2))
        return (ct[:, :, :, None, :] * eye[None, :, None, :, None]).reshape(DEPTH, g * p, g * gc)

    bmat = jnp.concatenate([to_bmat(bbr), to_bmat(bbi)], axis=-1).astype(bf16)
    cmat = jnp.concatenate([to_cmat(c_re), to_cmat(-c_im)], axis=-2).astype(bf16)
    lam = jnp.stack([lbr.reshape(DEPTH, g * p), lbi.reshape(DEPTH, g * p)], axis=1)
    return lam, bmat, cmat


def _pool_blockdiag(pool_w):
    k = len(POOL_WINDOWS)
    eye = jnp.eye(k, dtype=f32)
    w = pool_w[:, :, :, None, :] * eye[None, :, None, :, None]
    return w.reshape(DEPTH, MIX_W, MIX_W).astype(bf16)


def _time_major_state(s):
    return jnp.transpose(s, (0, 2, 1, 3))


def kernel(x_prompt, x_sample, state_pool, state_ssm_re, state_ssm_im, state_shortconv, state_conformer,
           norm1_g, norm2_g, final_g, w_in, b_gate, pool_w, pool_scale, lam_re, lam_im, log_dt,
           b_re, b_im, c_re, c_im, d_skip, w_glu, b_glu, sc_w, cf_w, cf_ln_g, cf_ln_b,
           w_branch, w_out, w_ffn_in, w_ffn_out):
    nb = x_prompt.shape[0]
    lam, bmat, cmat = _ssm_params(lam_re, lam_im, log_dt, b_re, b_im, c_re, c_im)
    row = lambda a: a.reshape(DEPTH, 1, a.shape[-1])
    taps = lambda w: jnp.broadcast_to(w[:, :, None, :], w.shape[:2] + (SUBLANES, MIX_W))
    mixer_w = (row(norm1_g), w_in.astype(bf16), b_gate, _pool_blockdiag(pool_w), row(pool_scale),
               lam, bmat, cmat, row(d_skip), w_glu.astype(bf16), row(b_glu), taps(sc_w), taps(cf_w),
               row(cf_ln_g), row(cf_ln_b), w_branch.astype(bf16), w_out.astype(bf16))
    n2 = row(norm2_g)
    wfi, wfo = w_ffn_in.astype(bf16), w_ffn_out.astype(bf16)
    fg = final_g.reshape(1, D_MODEL)

    groups = []
    zeros = lambda h: jnp.zeros((DEPTH, h, nb, MIX_W), f32)
    z_ssm = jnp.zeros((DEPTH, nb, SSM_W), f32)
    groups.append(dict(x=jnp.transpose(x_prompt, (1, 0, 2)), pos0=0, tt=64, nsb=nb,
                       st=(zeros(POOL_BUF), z_ssm, z_ssm, zeros(SC_WIDTH - 1), zeros(CF_WIDTH - 1))))
    ns = x_sample.shape[0]
    groups.append(dict(x=jnp.transpose(x_sample, (1, 0, 2)), pos0=PAST_LEN, tt=x_sample.shape[1], nsb=64,
                       st=(_time_major_state(state_pool), state_ssm_re.reshape(DEPTH, ns, SSM_W),
                           state_ssm_im.reshape(DEPTH, ns, SSM_W), _time_major_state(state_shortconv),
                           _time_major_state(state_conformer))))

    results = []
    for grp in groups:
        x3 = grp["x"]
        t_total, n_seq, _ = x3.shape
        outs = [[] for _ in range(5)]
        for layer in range(DEPTH):
            x3, po, reo, imo, sco, cfo = _mixer_call(layer, x3, grp["st"], mixer_w,
                                                     tt=grp["tt"], nsb=grp["nsb"], pos0=grp["pos0"])
            for lst, o in zip(outs, (po, reo, imo, sco, cfo)):
                lst.append(o)
            x2 = _ffn_call(layer, x3.reshape(t_total * n_seq, D_MODEL), n2, wfi, wfo, fg,
                           rows=512, final=(layer == DEPTH - 1))
            x3 = x2.reshape(t_total, n_seq, D_MODEL)
        y = jnp.transpose(x3, (1, 0, 2))
        pool = jnp.transpose(jnp.stack(outs[0]), (0, 2, 1, 3))
        sre = jnp.stack(outs[1]).reshape(DEPTH, n_seq, SSM_GROUPS, SSM_STATE)
        sim = jnp.stack(outs[2]).reshape(DEPTH, n_seq, SSM_GROUPS, SSM_STATE)
        sc = jnp.transpose(jnp.stack(outs[3]), (0, 2, 1, 3))
        cf = jnp.transpose(jnp.stack(outs[4]), (0, 2, 1, 3))
        results.append((y, pool, sre, sim, sc, cf))

    (yp, pp, rp, ip, cp, fp), (ys, ps, rs, is_, cs, fs) = results
    return (yp, ys, pp, ps, rp, rs, ip, is_, cp, cs, fp, fs)
```

```python
import functools
import math

import jax
import jax.numpy as jnp
from jax import lax
from jax.experimental import pallas as pl
from jax.experimental.pallas import tpu as pltpu

D_MODEL = 1024
DEPTH = 4
PAST_LEN = 16384
N_BRANCH = 4
MIX_W = D_MODEL // 4
POOL_WINDOWS = (2, 4, 8, 16)
POOL_GW = MIX_W // len(POOL_WINDOWS)
POOL_BUF = max(POOL_WINDOWS) - 1
SSM_GROUP_CH = 16
SSM_GROUPS = MIX_W // SSM_GROUP_CH
SSM_STATE = 64
SSM_W = SSM_GROUPS * SSM_STATE
SC_WIDTH = 3
CF_WIDTH = 31
FF_DIM = -(-8 * D_MODEL // (3 * 256)) * 256
OFF_POOL = 0
OFF_SSM = OFF_POOL + MIX_W
OFF_SC = OFF_SSM + MIX_W
OFF_CF = OFF_SC + 3 * MIX_W
OFF_GATE = OFF_CF + 2 * MIX_W
IN_COLS = OFF_GATE + N_BRANCH * D_MODEL
RMS_EPS = 1e-6
LN_EPS = 1e-5

SUBLANES = 8
VMEM_LIMIT_BYTES = 56 * 1024 * 1024
FFN_COL_CHUNKS = ((0, 1024), (1024, 2048), (2048, FF_DIM))
FFN_ROWS = 1024
LANES = 128
CONV_TIME_BLOCK = 8

f32 = jnp.float32
bf16 = jnp.bfloat16


def _sigmoid(x):
    return 0.5 * jnp.tanh(0.5 * x) + 0.5


def _gelu_tanh(x):
    c = math.sqrt(2.0 / math.pi)
    return x * (0.5 * (1.0 + jnp.tanh(c * (x + 0.044715 * (x * x * x)))))


def _rmsnorm(x, g):
    y = x * lax.rsqrt(jnp.mean(x * x, axis=-1, keepdims=True) + RMS_EPS)
    return y * g


def _dot(a, b):
    return jnp.dot(a, b, preferred_element_type=f32)


def _causal_conv(ext_s, w_ref, width, tt, nsb, dyn_zero):
    tb = min(tt, CONV_TIME_BLOCK)
    assert tt % tb == 0
    out0 = (width - 1 + tt) * nsb
    for l0 in range(0, MIX_W, LANES):
        for r0 in range(0, nsb, SUBLANES):
            for t0 in range(0, tt, tb):
                acc = [None] * tb
                for e in range(tb + width - 1):
                    row = pl.multiple_of((t0 + e) * nsb + r0 + dyn_zero, SUBLANES)
                    xe = ext_s[pl.ds(row, SUBLANES), l0:l0 + LANES]
                    for b in range(max(0, e - width + 1), min(tb, e + 1)):
                        term = w_ref[e - b, :, l0:l0 + LANES] * xe
                        acc[b] = term if acc[b] is None else acc[b] + term
                for b in range(tb):
                    row = pl.multiple_of(out0 + (t0 + b) * nsb + r0 + dyn_zero, SUBLANES)
                    ext_s[pl.ds(row, SUBLANES), l0:l0 + LANES] = acc[b]


def _mixer_kernel(zero_ref, x_ref, sp_ref, sre_ref, sim_ref, ssc_ref, scf_ref,
                  n1_ref, win_ref, bgate_ref, wpool_ref, pscale_ref, lam_ref, bmat_ref, cmat_ref,
                  dskip_ref, wglu_ref, bglu_ref, scw_ref, cfw_ref, lng_ref, lnb_ref, wbr_ref, wout_ref,
                  xo_ref, po_ref, reo_ref, imo_ref, sco_ref, cfo_ref,
                  bu_s, extp_s, exts_s, extc_s, h_s, *maybe_xt_s,
                  tt, nsb, pos0, seq_major_in):
    rows = tt * nsb
    hp, hs, hc = POOL_BUF * nsb, (SC_WIDTH - 1) * nsb, (CF_WIDTH - 1) * nsb
    t_idx = pl.program_id(1)
    dyn_zero = zero_ref[0]

    @pl.when(t_idx == 0)
    def _load_state():
        extp_s[0:hp, :] = sp_ref[...].reshape(hp, MIX_W)
        exts_s[0:hs, :] = ssc_ref[...].reshape(hs, MIX_W)
        extc_s[0:hc, :] = scf_ref[...].reshape(hc, MIX_W)
        h_s[:, 0:SSM_W] = sre_ref[...]
        h_s[:, SSM_W:] = sim_ref[...]

    if seq_major_in:
        (xt_s,) = maybe_xt_s
        for n in range(nsb):
            for j in range(D_MODEL // LANES):
                xt_s[j, pl.ds(n, tt, stride=nsb), :] = x_ref[n, :, j * LANES:(j + 1) * LANES]
        x = jnp.concatenate([xt_s[j] for j in range(D_MODEL // LANES)], axis=1)
    else:
        x = x_ref[...].reshape(rows, D_MODEL)
    hn = _rmsnorm(x, n1_ref[...]).astype(bf16)

    def proj(lo, hi):
        return _dot(hn, win_ref[:, lo:hi])

    def gate(k):
        lo = OFF_GATE + k * D_MODEL
        return _sigmoid(proj(lo, lo + D_MODEL) + bgate_ref[k:k + 1, :])

    vu = proj(OFF_POOL, OFF_SC)
    v, u = vu[:, 0:MIX_W], vu[:, MIX_W:]
    ab = proj(OFF_CF, OFF_GATE)
    bu_s[...] = _dot(u.astype(bf16), bmat_ref[...])
    scp = proj(OFF_SC, OFF_CF)
    gate_c = gate(2)
    gate_a = gate(0)

    extc_s[hc:hc + rows, :] = ab[:, 0:MIX_W] * _sigmoid(ab[:, MIX_W:])
    _causal_conv(extc_s, cfw_ref, CF_WIDTH, tt, nsb, dyn_zero)
    new_cf = extc_s[rows:rows + hc, :]
    extc_s[0:hc, :] = new_cf
    cfo_ref[...] = new_cf.reshape(CF_WIDTH - 1, nsb, MIX_W)

    exts_s[hs:hs + rows, :] = scp[:, MIX_W:2 * MIX_W] * scp[:, 2 * MIX_W:]
    _causal_conv(exts_s, scw_ref, SC_WIDTH, tt, nsb, dyn_zero)
    yc = scp[:, 0:MIX_W] * exts_s[hs + rows:hs + 2 * rows, :]
    new_sc = exts_s[rows:rows + hs, :]
    exts_s[0:hs, :] = new_sc
    sco_ref[...] = new_sc.reshape(SC_WIDTH - 1, nsb, MIX_W)

    extp_s[hp:hp + rows, :] = v
    ext = extp_s[...]
    s2 = ext[nsb:, :] + ext[:-nsb, :]
    s4 = s2[2 * nsb:, :] + s2[:-2 * nsb, :]
    s8 = s4[4 * nsb:, :] + s4[:-4 * nsb, :]
    s16 = s8[8 * nsb:, :] + s8[:-8 * nsb, :]
    lane = lax.broadcasted_iota(jnp.int32, (1, MIX_W), 1)
    grp = lane // POOL_GW
    wsum = jnp.where(grp == 0, s2[14 * nsb:, :],
                     jnp.where(grp == 1, s4[12 * nsb:, :],
                               jnp.where(grp == 2, s8[8 * nsb:, :], s16)))
    win = jnp.where(grp == 0, 2.0, jnp.where(grp == 1, 4.0, jnp.where(grp == 2, 8.0, 16.0))).astype(f32)
    step = (lax.broadcasted_iota(jnp.int32, (rows, MIX_W), 0) >> (nsb.bit_length() - 1)) + t_idx * tt
    pos = step.astype(f32) + float(pos0)
    cnt = jnp.minimum(pos + 1.0, win)
    d = wsum / cnt - v
    ya = _dot(d.astype(bf16), wpool_ref[...]) * pscale_ref[...]
    new_pool = extp_s[rows:rows + hp, :]
    extp_s[0:hp, :] = new_pool
    po_ref[...] = new_pool.reshape(POOL_BUF, nsb, MIX_W)

    merged = gate_a * _dot(ya.astype(bf16), wbr_ref[0]) + gate_c * _dot(yc.astype(bf16), wbr_ref[2])
    gate_d = gate(3)

    lam_r = jnp.broadcast_to(lam_ref[0:1, :], (SUBLANES, SSM_W))
    lam_i = jnp.broadcast_to(lam_ref[1:2, :], (SUBLANES, SSM_W))
    for j in range(nsb // SUBLANES):
        r0 = j * SUBLANES
        hr, hi = h_s[r0:r0 + SUBLANES, 0:SSM_W], h_s[r0:r0 + SUBLANES, SSM_W:]
        for t in range(tt):
            row = t * nsb + r0
            nr = lam_r * hr - lam_i * hi + bu_s[row:row + SUBLANES, 0:SSM_W]
            ni = lam_r * hi + lam_i * hr + bu_s[row:row + SUBLANES, SSM_W:]
            bu_s[row:row + SUBLANES, 0:SSM_W] = nr
            bu_s[row:row + SUBLANES, SSM_W:] = ni
            hr, hi = nr, ni
        h_s[r0:r0 + SUBLANES, 0:SSM_W] = hr
        h_s[r0:r0 + SUBLANES, SSM_W:] = hi
    reo_ref[...] = h_s[:, 0:SSM_W]
    imo_ref[...] = h_s[:, SSM_W:]
    y = (_dot(bu_s[:, 0:SSM_W].astype(bf16), cmat_ref[0:SSM_W, :])
         + _dot(bu_s[:, SSM_W:].astype(bf16), cmat_ref[SSM_W:, :]) + dskip_ref[...] * u)

    conv = extc_s[hc + rows:hc + 2 * rows, :]
    mu = jnp.mean(conv, axis=-1, keepdims=True)
    cen = conv - mu
    var = jnp.mean(cen * cen, axis=-1, keepdims=True)
    yn = cen * lax.rsqrt(var + LN_EPS) * lng_ref[...] + lnb_ref[...]
    yd = yn * _sigmoid(yn)
    merged = merged + gate_d * _dot(yd.astype(bf16), wbr_ref[3])
    gate_b = gate(1)

    z = _gelu_tanh(y)
    yb = z * _sigmoid(_dot(z.astype(bf16), wglu_ref[...]) + bglu_ref[...])
    merged = merged + gate_b * _dot(yb.astype(bf16), wbr_ref[1])

    xo = x + _dot(merged.astype(bf16), wout_ref[...])
    xo_ref[...] = xo.reshape(tt, nsb, D_MODEL)


def _const_spec(block_shape, index):
    return pl.BlockSpec(block_shape, lambda s, t: index, pipeline_mode=pl.Buffered(1))


def _mixer_call(layer, x3, st, wts, *, tt, nsb, pos0, seq_major_in=False):
    if seq_major_in:
        n_seq, t_total, _ = x3.shape
    else:
        t_total, n_seq, _ = x3.shape
    assert nsb % SUBLANES == 0 and nsb & (nsb - 1) == 0 and n_seq % nsb == 0 and t_total % tt == 0
    rows = tt * nsb
    grid = (n_seq // nsb, t_total // tt)
    sp, sre, sim, ssc, scf = st
    c = MIX_W

    def hist_spec(h):
        return pl.BlockSpec((None, h, nsb, c), lambda s, t: (layer, 0, s, 0))

    ssm_spec = pl.BlockSpec((None, nsb, SSM_W), lambda s, t: (layer, s, 0))
    x_spec = pl.BlockSpec((tt, nsb, D_MODEL), lambda s, t: (t, s, 0))
    x_in_spec = pl.BlockSpec((nsb, tt, D_MODEL), lambda s, t: (s, t, 0)) if seq_major_in else x_spec
    l3 = (layer, 0, 0)
    in_specs = [
        pl.BlockSpec(memory_space=pltpu.SMEM),
        x_in_spec, hist_spec(POOL_BUF), ssm_spec, ssm_spec, hist_spec(SC_WIDTH - 1), hist_spec(CF_WIDTH - 1),
        _const_spec((None, 1, D_MODEL), l3),
        _const_spec((None, D_MODEL, IN_COLS), l3),
        _const_spec((None, N_BRANCH, D_MODEL), l3),
        _const_spec((None, c, c), l3),
        _const_spec((None, 1, c), l3),
        _const_spec((None, 2, SSM_W), l3),
        _const_spec((None, c, 2 * SSM_W), l3),
        _const_spec((None, 2 * SSM_W, c), l3),
        _const_spec((None, 1, c), l3),
        _const_spec((None, c, c), l3),
        _const_spec((None, 1, c), l3),
        _const_spec((None, SC_WIDTH, SUBLANES, c), (layer, 0, 0, 0)),
        _const_spec((None, CF_WIDTH, SUBLANES, c), (layer, 0, 0, 0)),
        _const_spec((None, 1, c), l3),
        _const_spec((None, 1, c), l3),
        _const_spec((None, N_BRANCH, c, D_MODEL), (layer, 0, 0, 0)),
        _const_spec((None, D_MODEL, D_MODEL), l3),
    ]

    def hist_out(h):
        return pl.BlockSpec((h, nsb, c), lambda s, t: (0, s, 0))

    ssm_out = pl.BlockSpec((nsb, SSM_W), lambda s, t: (s, 0))
    out_specs = [x_spec, hist_out(POOL_BUF), ssm_out, ssm_out, hist_out(SC_WIDTH - 1), hist_out(CF_WIDTH - 1)]
    out_shape = [
        jax.ShapeDtypeStruct((t_total, n_seq, D_MODEL), f32),
        jax.ShapeDtypeStruct((POOL_BUF, n_seq, c), f32),
        jax.ShapeDtypeStruct((n_seq, SSM_W), f32),
        jax.ShapeDtypeStruct((n_seq, SSM_W), f32),
        jax.ShapeDtypeStruct((SC_WIDTH - 1, n_seq, c), f32),
        jax.ShapeDtypeStruct((CF_WIDTH - 1, n_seq, c), f32),
    ]
    scratch = [
        pltpu.VMEM((rows, 2 * SSM_W), f32),
        pltpu.VMEM((POOL_BUF * nsb + rows, c), f32),
        pltpu.VMEM(((SC_WIDTH - 1) * nsb + 2 * rows, c), f32),
        pltpu.VMEM(((CF_WIDTH - 1) * nsb + 2 * rows, c), f32),
        pltpu.VMEM((nsb, 2 * SSM_W), f32),
    ]
    if seq_major_in:
        scratch.append(pltpu.VMEM((D_MODEL // LANES, rows, LANES), f32))
    kern = functools.partial(_mixer_kernel, tt=tt, nsb=nsb, pos0=pos0, seq_major_in=seq_major_in)
    return pl.pallas_call(
        kern, grid=grid, in_specs=in_specs, out_specs=out_specs, out_shape=out_shape,
        scratch_shapes=scratch,
        compiler_params=pltpu.CompilerParams(dimension_semantics=("arbitrary", "arbitrary"),
                                             vmem_limit_bytes=VMEM_LIMIT_BYTES),
        name=f"mixer_l{layer}_n{nsb}",
    )(jnp.zeros((1,), jnp.int32), x3, sp, sre, sim, ssc, scf, *wts)


def _ffn_kernel(x_ref, n2_ref, wfi_ref, wfo_ref, fg_ref, o_ref, *maybe_t_s, final, seq_major_out):
    x = x_ref[...]
    hn = _rmsnorm(x, n2_ref[...]).astype(bf16)
    acc = x
    for lo, hi in FFN_COL_CHUNKS:
        g = _dot(hn, wfi_ref[:, lo:hi])
        up = _dot(hn, wfi_ref[:, FF_DIM + lo:FF_DIM + hi])
        act = (g * _sigmoid(g)) * up
        acc = acc + _dot(act.astype(bf16), wfo_ref[lo:hi, :])
    if final:
        acc = _rmsnorm(acc, fg_ref[...])
    if seq_major_out:
        (t_s,) = maybe_t_s
        n_seq, steps, _ = o_ref.shape
        for j in range(D_MODEL // LANES):
            t_s[j] = acc[:, j * LANES:(j + 1) * LANES]
        for n in range(n_seq):
            for j in range(D_MODEL // LANES):
                o_ref[n, :, j * LANES:(j + 1) * LANES] = t_s[j, pl.ds(n, steps, stride=n_seq), :]
    else:
        o_ref[...] = acc


def _ffn_call(layer, x2, n2, wfi, wfo, fg, *, rows, final, seq_major_out=0):
    total = x2.shape[0]
    x_spec = pl.BlockSpec((rows, D_MODEL), lambda r: (r, 0))
    if seq_major_out:
        steps = rows // seq_major_out
        out_spec = pl.BlockSpec((seq_major_out, steps, D_MODEL), lambda r: (0, r, 0))
        out_shape = jax.ShapeDtypeStruct((seq_major_out, total // seq_major_out, D_MODEL), f32)
        scratch = [pltpu.VMEM((D_MODEL // LANES, rows, LANES), f32)]
    else:
        out_spec, out_shape, scratch = x_spec, jax.ShapeDtypeStruct(x2.shape, f32), []

    def cspec(block_shape, index):
        return pl.BlockSpec(block_shape, lambda r: index, pipeline_mode=pl.Buffered(1))

    return pl.pallas_call(
        functools.partial(_ffn_kernel, final=final, seq_major_out=seq_major_out),
        grid=(total // rows,),
        in_specs=[x_spec,
                  cspec((None, 1, D_MODEL), (layer, 0, 0)),
                  cspec((None, D_MODEL, 2 * FF_DIM), (layer, 0, 0)),
                  cspec((None, FF_DIM, D_MODEL), (layer, 0, 0)),
                  cspec((1, D_MODEL), (0, 0))],
        out_specs=out_spec,
        out_shape=out_shape,
        scratch_shapes=scratch,
        compiler_params=pltpu.CompilerParams(dimension_semantics=("arbitrary",),
                                             vmem_limit_bytes=VMEM_LIMIT_BYTES),
        name=f"ffn_l{layer}_r{total}",
    )(x2, n2, wfi, wfo, fg)


def _ssm_params(lam_re, lam_im, log_dt, b_re, b_im, c_re, c_im):
    g, p, gc = SSM_GROUPS, SSM_STATE, SSM_GROUP_CH
    dt = jnp.exp(log_dt)[..., None]
    mag = jnp.exp(lam_re * dt)
    lbr, lbi = mag * jnp.cos(lam_im * dt), mag * jnp.sin(lam_im * dt)
    den = lam_re * lam_re + lam_im * lam_im
    qr = ((lbr - 1.0) * lam_re + lbi * lam_im) / den
    qi = (lbi * lam_re - (lbr - 1.0) * lam_im) / den
    bbr = qr[..., None] * b_re - qi[..., None] * b_im
    bbi = qr[..., None] * b_im + qi[..., None] * b_re
    eye = jnp.eye(g, dtype=f32)

    def to_bmat(b):
        bt = jnp.transpose(b, (0, 1, 3, 2))
        return (bt[:, :, :, None, :] * eye[None, :, None, :, None]).reshape(DEPTH, g * gc, g * p)

    def to_cmat(cm):
        ct = jnp.transpose(cm, (0, 1, 3, 2))
        return (ct[:, :, :, None, :] * eye[None, :, None, :, None]).reshape(DEPTH, g * p, g * gc)

    bmat = jnp.concatenate([to_bmat(bbr), to_bmat(bbi)], axis=-1).astype(bf16)
    cmat = jnp.concatenate([to_cmat(c_re), to_cmat(-c_im)], axis=-2).astype(bf16)
    lam = jnp.stack([lbr.reshape(DEPTH, g * p), lbi.reshape(DEPTH, g * p)], axis=1)
    return lam, bmat, cmat


def _pool_blockdiag(pool_w):
    k = len(POOL_WINDOWS)
    eye = jnp.eye(k, dtype=f32)
    w = pool_w[:, :, :, None, :] * eye[None, :, None, :, None]
    return w.reshape(DEPTH, MIX_W, MIX_W).astype(bf16)


def _time_major_state(s):
    return jnp.transpose(s, (0, 2, 1, 3))


def kernel(x_prompt, x_sample, state_pool, state_ssm_re, state_ssm_im, state_shortconv, state_conformer,
           norm1_g, norm2_g, final_g, w_in, b_gate, pool_w, pool_scale, lam_re, lam_im, log_dt,
           b_re, b_im, c_re, c_im, d_skip, w_glu, b_glu, sc_w, cf_w, cf_ln_g, cf_ln_b,
           w_branch, w_out, w_ffn_in, w_ffn_out):
    nb = x_prompt.shape[0]
    lam, bmat, cmat = _ssm_params(lam_re, lam_im, log_dt, b_re, b_im, c_re, c_im)
    row = lambda a: a.reshape(DEPTH, 1, a.shape[-1])
    taps = lambda w: jnp.broadcast_to(w[:, :, None, :], w.shape[:2] + (SUBLANES, MIX_W))
    mixer_w = (row(norm1_g), w_in.astype(bf16), b_gate, _pool_blockdiag(pool_w), row(pool_scale),
               lam, bmat, cmat, row(d_skip), w_glu.astype(bf16), row(b_glu), taps(sc_w), taps(cf_w),
               row(cf_ln_g), row(cf_ln_b), w_branch.astype(bf16), w_out.astype(bf16))
    n2 = row(norm2_g)
    wfi, wfo = w_ffn_in.astype(bf16), w_ffn_out.astype(bf16)
    fg = final_g.reshape(1, D_MODEL)

    groups = []
    zeros = lambda h: jnp.zeros((DEPTH, h, nb, MIX_W), f32)
    z_ssm = jnp.zeros((DEPTH, nb, SSM_W), f32)
    groups.append(dict(x=x_prompt, seq_major=True, pos0=0, tt=64, nsb=nb,
                       st=(zeros(POOL_BUF), z_ssm, z_ssm, zeros(SC_WIDTH - 1), zeros(CF_WIDTH - 1))))
    ns = x_sample.shape[0]
    groups.append(dict(x=jnp.transpose(x_sample, (1, 0, 2)), seq_major=False, pos0=PAST_LEN,
                       tt=x_sample.shape[1], nsb=64,
                       st=(_time_major_state(state_pool), state_ssm_re.reshape(DEPTH, ns, SSM_W),
                           state_ssm_im.reshape(DEPTH, ns, SSM_W), _time_major_state(state_shortconv),
                           _time_major_state(state_conformer))))

    results = []
    for grp in groups:
        x3, seq_major = grp["x"], grp["seq_major"]
        outs = [[] for _ in range(5)]
        for layer in range(DEPTH):
            last = layer == DEPTH - 1
            x3, po, reo, imo, sco, cfo = _mixer_call(layer, x3, grp["st"], mixer_w, tt=grp["tt"], nsb=grp["nsb"],
                                                     pos0=grp["pos0"], seq_major_in=seq_major and layer == 0)
            t_total, n_seq, _ = x3.shape
            for lst, o in zip(outs, (po, reo, imo, sco, cfo)):
                lst.append(o)
            x2 = _ffn_call(layer, x3.reshape(t_total * n_seq, D_MODEL), n2, wfi, wfo, fg, rows=FFN_ROWS,
                           final=last, seq_major_out=n_seq if (seq_major and last) else 0)
            x3 = x2 if (seq_major and last) else x2.reshape(t_total, n_seq, D_MODEL)
        y = x3 if seq_major else jnp.transpose(x3, (1, 0, 2))
        pool = jnp.transpose(jnp.stack(outs[0]), (0, 2, 1, 3))
        sre = jnp.stack(outs[1]).reshape(DEPTH, n_seq, SSM_GROUPS, SSM_STATE)
        sim = jnp.stack(outs[2]).reshape(DEPTH, n_seq, SSM_GROUPS, SSM_STATE)
        sc = jnp.transpose(jnp.stack(outs[3]), (0, 2, 1, 3))
        cf = jnp.transpose(jnp.stack(outs[4]), (0, 2, 1, 3))
        results.append((y, pool, sre, sim, sc, cf))

    (yp, pp, rp, ip, cp, fp), (ys, ps, rs, is_, cs, fs) = results
    return (yp, ys, pp, ps, rp, rs, ip, is_, cp, cs, fp, fs)
```

```python
import functools
import math

import jax
import jax.numpy as jnp
from jax import lax
from jax.experimental import pallas as pl
from jax.experimental.pallas import tpu as pltpu

D_MODEL = 1024
DEPTH = 4
PAST_LEN = 16384
N_BRANCH = 4
MIX_W = D_MODEL // 4
POOL_WINDOWS = (2, 4, 8, 16)
POOL_GW = MIX_W // len(POOL_WINDOWS)
POOL_BUF = max(POOL_WINDOWS) - 1
SSM_GROUP_CH = 16
SSM_GROUPS = MIX_W // SSM_GROUP_CH
SSM_STATE = 64
SSM_W = SSM_GROUPS * SSM_STATE
SC_WIDTH = 3
CF_WIDTH = 31
FF_DIM = -(-8 * D_MODEL // (3 * 256)) * 256
OFF_POOL = 0
OFF_SSM = OFF_POOL + MIX_W
OFF_SC = OFF_SSM + MIX_W
OFF_CF = OFF_SC + 3 * MIX_W
OFF_GATE = OFF_CF + 2 * MIX_W
IN_COLS = OFF_GATE + N_BRANCH * D_MODEL
RMS_EPS = 1e-6
LN_EPS = 1e-5

SUBLANES = 8
VMEM_LIMIT_BYTES = 56 * 1024 * 1024
FFN_COL_CHUNKS = ((0, 1024), (1024, 2048), (2048, FF_DIM))
FFN_ROWS = 1024
LANES = 128
CONV_TIME_BLOCK = 2

f32 = jnp.float32
bf16 = jnp.bfloat16


def _sigmoid(x):
    return 0.5 * jnp.tanh(0.5 * x) + 0.5


def _gelu_tanh(x):
    c = math.sqrt(2.0 / math.pi)
    return x * (0.5 * (1.0 + jnp.tanh(c * (x + 0.044715 * (x * x * x)))))


def _rmsnorm(x, g):
    y = x * lax.rsqrt(jnp.mean(x * x, axis=-1, keepdims=True) + RMS_EPS)
    return y * g


def _dot(a, b):
    return jnp.dot(a, b, preferred_element_type=f32)


def _causal_conv(ext_s, w_ref, width, tt, nsb, dyn_zero):
    tb = min(tt, CONV_TIME_BLOCK)
    assert tt % tb == 0
    out0 = (width - 1 + tt) * nsb
    for l0 in range(0, MIX_W, LANES):
        for r0 in range(0, nsb, SUBLANES):
            for t0 in range(0, tt, tb):
                acc = [None] * tb
                for e in range(tb + width - 1):
                    row = pl.multiple_of((t0 + e) * nsb + r0 + dyn_zero, SUBLANES)
                    xe = ext_s[pl.ds(row, SUBLANES), l0:l0 + LANES]
                    for b in range(max(0, e - width + 1), min(tb, e + 1)):
                        term = w_ref[e - b, :, l0:l0 + LANES] * xe
                        acc[b] = term if acc[b] is None else acc[b] + term
                for b in range(tb):
                    row = pl.multiple_of(out0 + (t0 + b) * nsb + r0 + dyn_zero, SUBLANES)
                    ext_s[pl.ds(row, SUBLANES), l0:l0 + LANES] = acc[b]


def _mixer_kernel(zero_ref, x_ref, sp_ref, sre_ref, sim_ref, ssc_ref, scf_ref,
                  n1_ref, win_ref, bgate_ref, wpool_ref, pscale_ref, lam_ref, bmat_ref, cmat_ref,
                  dskip_ref, wglu_ref, bglu_ref, scw_ref, cfw_ref, lng_ref, lnb_ref, wbr_ref, wout_ref,
                  xo_ref, po_ref, reo_ref, imo_ref, sco_ref, cfo_ref,
                  bu_s, extp_s, exts_s, extc_s, h_s, *maybe_xt_s,
                  tt, nsb, pos0, seq_major_in):
    rows = tt * nsb
    hp, hs, hc = POOL_BUF * nsb, (SC_WIDTH - 1) * nsb, (CF_WIDTH - 1) * nsb
    t_idx = pl.program_id(1)
    dyn_zero = zero_ref[0]

    @pl.when(t_idx == 0)
    def _load_state():
        extp_s[0:hp, :] = sp_ref[...].reshape(hp, MIX_W)
        exts_s[0:hs, :] = ssc_ref[...].reshape(hs, MIX_W)
        extc_s[0:hc, :] = scf_ref[...].reshape(hc, MIX_W)
        h_s[:, 0:SSM_W] = sre_ref[...]
        h_s[:, SSM_W:] = sim_ref[...]

    if seq_major_in:
        (xt_s,) = maybe_xt_s
        for n in range(nsb):
            for j in range(D_MODEL // LANES):
                xt_s[j, pl.ds(n, tt, stride=nsb), :] = x_ref[n, :, j * LANES:(j + 1) * LANES]
        x = jnp.concatenate([xt_s[j] for j in range(D_MODEL // LANES)], axis=1)
    else:
        x = x_ref[...].reshape(rows, D_MODEL)
    hn = _rmsnorm(x, n1_ref[...]).astype(bf16)

    def proj(lo, hi):
        return _dot(hn, win_ref[:, lo:hi])

    def gate(k):
        lo = OFF_GATE + k * D_MODEL
        return _sigmoid(proj(lo, lo + D_MODEL) + bgate_ref[k:k + 1, :])

    vu = proj(OFF_POOL, OFF_SC)
    v, u = vu[:, 0:MIX_W], vu[:, MIX_W:]
    ab = proj(OFF_CF, OFF_GATE)
    bu_s[...] = _dot(u.astype(bf16), bmat_ref[...])
    scp = proj(OFF_SC, OFF_CF)
    gate_c = gate(2)
    gate_a = gate(0)

    extc_s[hc:hc + rows, :] = ab[:, 0:MIX_W] * _sigmoid(ab[:, MIX_W:])
    _causal_conv(extc_s, cfw_ref, CF_WIDTH, tt, nsb, dyn_zero)
    new_cf = extc_s[rows:rows + hc, :]
    extc_s[0:hc, :] = new_cf
    cfo_ref[...] = new_cf.reshape(CF_WIDTH - 1, nsb, MIX_W)

    exts_s[hs:hs + rows, :] = scp[:, MIX_W:2 * MIX_W] * scp[:, 2 * MIX_W:]
    _causal_conv(exts_s, scw_ref, SC_WIDTH, tt, nsb, dyn_zero)
    yc = scp[:, 0:MIX_W] * exts_s[hs + rows:hs + 2 * rows, :]
    new_sc = exts_s[rows:rows + hs, :]
    exts_s[0:hs, :] = new_sc
    sco_ref[...] = new_sc.reshape(SC_WIDTH - 1, nsb, MIX_W)

    extp_s[hp:hp + rows, :] = v
    ext = extp_s[...]
    s2 = ext[nsb:, :] + ext[:-nsb, :]
    s4 = s2[2 * nsb:, :] + s2[:-2 * nsb, :]
    s8 = s4[4 * nsb:, :] + s4[:-4 * nsb, :]
    s16 = s8[8 * nsb:, :] + s8[:-8 * nsb, :]
    lane = lax.broadcasted_iota(jnp.int32, (1, MIX_W), 1)
    grp = lane // POOL_GW
    wsum = jnp.where(grp == 0, s2[14 * nsb:, :],
                     jnp.where(grp == 1, s4[12 * nsb:, :],
                               jnp.where(grp == 2, s8[8 * nsb:, :], s16)))
    win = jnp.where(grp == 0, 2.0, jnp.where(grp == 1, 4.0, jnp.where(grp == 2, 8.0, 16.0))).astype(f32)
    step = (lax.broadcasted_iota(jnp.int32, (rows, MIX_W), 0) >> (nsb.bit_length() - 1)) + t_idx * tt
    pos = step.astype(f32) + float(pos0)
    cnt = jnp.minimum(pos + 1.0, win)
    d = wsum / cnt - v
    ya = _dot(d.astype(bf16), wpool_ref[...]) * pscale_ref[...]
    new_pool = extp_s[rows:rows + hp, :]
    extp_s[0:hp, :] = new_pool
    po_ref[...] = new_pool.reshape(POOL_BUF, nsb, MIX_W)

    merged = gate_a * _dot(ya.astype(bf16), wbr_ref[0]) + gate_c * _dot(yc.astype(bf16), wbr_ref[2])
    gate_d = gate(3)

    lam_r = jnp.broadcast_to(lam_ref[0:1, :], (SUBLANES, SSM_W))
    lam_i = jnp.broadcast_to(lam_ref[1:2, :], (SUBLANES, SSM_W))
    for j in range(nsb // SUBLANES):
        r0 = j * SUBLANES
        hr, hi = h_s[r0:r0 + SUBLANES, 0:SSM_W], h_s[r0:r0 + SUBLANES, SSM_W:]
        for t in range(tt):
            row = t * nsb + r0
            nr = lam_r * hr - lam_i * hi + bu_s[row:row + SUBLANES, 0:SSM_W]
            ni = lam_r * hi + lam_i * hr + bu_s[row:row + SUBLANES, SSM_W:]
            bu_s[row:row + SUBLANES, 0:SSM_W] = nr
            bu_s[row:row + SUBLANES, SSM_W:] = ni
            hr, hi = nr, ni
        h_s[r0:r0 + SUBLANES, 0:SSM_W] = hr
        h_s[r0:r0 + SUBLANES, SSM_W:] = hi
    reo_ref[...] = h_s[:, 0:SSM_W]
    imo_ref[...] = h_s[:, SSM_W:]
    y = (_dot(bu_s[:, 0:SSM_W].astype(bf16), cmat_ref[0:SSM_W, :])
         + _dot(bu_s[:, SSM_W:].astype(bf16), cmat_ref[SSM_W:, :]) + dskip_ref[...] * u)

    conv = extc_s[hc + rows:hc + 2 * rows, :]
    mu = jnp.mean(conv, axis=-1, keepdims=True)
    cen = conv - mu
    var = jnp.mean(cen * cen, axis=-1, keepdims=True)
    yn = cen * lax.rsqrt(var + LN_EPS) * lng_ref[...] + lnb_ref[...]
    yd = yn * _sigmoid(yn)
    merged = merged + gate_d * _dot(yd.astype(bf16), wbr_ref[3])
    gate_b = gate(1)

    z = _gelu_tanh(y)
    yb = z * _sigmoid(_dot(z.astype(bf16), wglu_ref[...]) + bglu_ref[...])
    merged = merged + gate_b * _dot(yb.astype(bf16), wbr_ref[1])

    xo = x + _dot(merged.astype(bf16), wout_ref[...])
    xo_ref[...] = xo.reshape(tt, nsb, D_MODEL)


def _const_spec(block_shape, index):
    return pl.BlockSpec(block_shape, lambda s, t: index, pipeline_mode=pl.Buffered(1))


def _mixer_call(layer, x3, st, wts, *, tt, nsb, pos0, seq_major_in=False):
    if seq_major_in:
        n_seq, t_total, _ = x3.shape
    else:
        t_total, n_seq, _ = x3.shape
    assert nsb % SUBLANES == 0 and nsb & (nsb - 1) == 0 and n_seq % nsb == 0 and t_total % tt == 0
    rows = tt * nsb
    grid = (n_seq // nsb, t_total // tt)
    sp, sre, sim, ssc, scf = st
    c = MIX_W

    def hist_spec(h):
        return pl.BlockSpec((None, h, nsb, c), lambda s, t: (layer, 0, s, 0))

    ssm_spec = pl.BlockSpec((None, nsb, SSM_W), lambda s, t: (layer, s, 0))
    x_spec = pl.BlockSpec((tt, nsb, D_MODEL), lambda s, t: (t, s, 0))
    x_in_spec = pl.BlockSpec((nsb, tt, D_MODEL), lambda s, t: (s, t, 0)) if seq_major_in else x_spec
    l3 = (layer, 0, 0)
    in_specs = [
        pl.BlockSpec(memory_space=pltpu.SMEM),
        x_in_spec, hist_spec(POOL_BUF), ssm_spec, ssm_spec, hist_spec(SC_WIDTH - 1), hist_spec(CF_WIDTH - 1),
        _const_spec((None, 1, D_MODEL), l3),
        _const_spec((None, D_MODEL, IN_COLS), l3),
        _const_spec((None, N_BRANCH, D_MODEL), l3),
        _const_spec((None, c, c), l3),
        _const_spec((None, 1, c), l3),
        _const_spec((None, 2, SSM_W), l3),
        _const_spec((None, c, 2 * SSM_W), l3),
        _const_spec((None, 2 * SSM_W, c), l3),
        _const_spec((None, 1, c), l3),
        _const_spec((None, c, c), l3),
        _const_spec((None, 1, c), l3),
        _const_spec((None, SC_WIDTH, SUBLANES, c), (layer, 0, 0, 0)),
        _const_spec((None, CF_WIDTH, SUBLANES, c), (layer, 0, 0, 0)),
        _const_spec((None, 1, c), l3),
        _const_spec((None, 1, c), l3),
        _const_spec((None, N_BRANCH, c, D_MODEL), (layer, 0, 0, 0)),
        _const_spec((None, D_MODEL, D_MODEL), l3),
    ]

    def hist_out(h):
        return pl.BlockSpec((h, nsb, c), lambda s, t: (0, s, 0))

    ssm_out = pl.BlockSpec((nsb, SSM_W), lambda s, t: (s, 0))
    out_specs = [x_spec, hist_out(POOL_BUF), ssm_out, ssm_out, hist_out(SC_WIDTH - 1), hist_out(CF_WIDTH - 1)]
    out_shape = [
        jax.ShapeDtypeStruct((t_total, n_seq, D_MODEL), f32),
        jax.ShapeDtypeStruct((POOL_BUF, n_seq, c), f32),
        jax.ShapeDtypeStruct((n_seq, SSM_W), f32),
        jax.ShapeDtypeStruct((n_seq, SSM_W), f32),
        jax.ShapeDtypeStruct((SC_WIDTH - 1, n_seq, c), f32),
        jax.ShapeDtypeStruct((CF_WIDTH - 1, n_seq, c), f32),
    ]
    scratch = [
        pltpu.VMEM((rows, 2 * SSM_W), f32),
        pltpu.VMEM((POOL_BUF * nsb + rows, c), f32),
        pltpu.VMEM(((SC_WIDTH - 1) * nsb + 2 * rows, c), f32),
        pltpu.VMEM(((CF_WIDTH - 1) * nsb + 2 * rows, c), f32),
        pltpu.VMEM((nsb, 2 * SSM_W), f32),
    ]
    if seq_major_in:
        scratch.append(pltpu.VMEM((D_MODEL // LANES, rows, LANES), f32))
    kern = functools.partial(_mixer_kernel, tt=tt, nsb=nsb, pos0=pos0, seq_major_in=seq_major_in)
    return pl.pallas_call(
        kern, grid=grid, in_specs=in_specs, out_specs=out_specs, out_shape=out_shape,
        scratch_shapes=scratch,
        compiler_params=pltpu.CompilerParams(dimension_semantics=("arbitrary", "arbitrary"),
                                             vmem_limit_bytes=VMEM_LIMIT_BYTES),
        name=f"mixer_l{layer}_n{nsb}",
    )(jnp.zeros((1,), jnp.int32), x3, sp, sre, sim, ssc, scf, *wts)


def _ffn_kernel(x_ref, n2_ref, wfi_ref, wfo_ref, fg_ref, o_ref, *maybe_t_s, final, seq_major_out):
    x = x_ref[...]
    hn = _rmsnorm(x, n2_ref[...]).astype(bf16)
    acc = x
    for lo, hi in FFN_COL_CHUNKS:
        g = _dot(hn, wfi_ref[:, lo:hi])
        up = _dot(hn, wfi_ref[:, FF_DIM + lo:FF_DIM + hi])
        act = (g * _sigmoid(g)) * up
        acc = acc + _dot(act.astype(bf16), wfo_ref[lo:hi, :])
    if final:
        acc = _rmsnorm(acc, fg_ref[...])
    if seq_major_out:
        (t_s,) = maybe_t_s
        n_seq, steps, _ = o_ref.shape
        for j in range(D_MODEL // LANES):
            t_s[j] = acc[:, j * LANES:(j + 1) * LANES]
        for n in range(n_seq):
            for j in range(D_MODEL // LANES):
                o_ref[n, :, j * LANES:(j + 1) * LANES] = t_s[j, pl.ds(n, steps, stride=n_seq), :]
    else:
        o_ref[...] = acc


def _ffn_call(layer, x2, n2, wfi, wfo, fg, *, rows, final, seq_major_out=0):
    total = x2.shape[0]
    x_spec = pl.BlockSpec((rows, D_MODEL), lambda r: (r, 0))
    if seq_major_out:
        steps = rows // seq_major_out
        out_spec = pl.BlockSpec((seq_major_out, steps, D_MODEL), lambda r: (0, r, 0))
        out_shape = jax.ShapeDtypeStruct((seq_major_out, total // seq_major_out, D_MODEL), f32)
        scratch = [pltpu.VMEM((D_MODEL // LANES, rows, LANES), f32)]
    else:
        out_spec, out_shape, scratch = x_spec, jax.ShapeDtypeStruct(x2.shape, f32), []

    def cspec(block_shape, index):
        return pl.BlockSpec(block_shape, lambda r: index, pipeline_mode=pl.Buffered(1))

    return pl.pallas_call(
        functools.partial(_ffn_kernel, final=final, seq_major_out=seq_major_out),
        grid=(total // rows,),
        in_specs=[x_spec,
                  cspec((None, 1, D_MODEL), (layer, 0, 0)),
                  cspec((None, D_MODEL, 2 * FF_DIM), (layer, 0, 0)),
                  cspec((None, FF_DIM, D_MODEL), (layer, 0, 0)),
                  cspec((1, D_MODEL), (0, 0))],
        out_specs=out_spec,
        out_shape=out_shape,
        scratch_shapes=scratch,
        compiler_params=pltpu.CompilerParams(dimension_semantics=("arbitrary",),
                                             vmem_limit_bytes=VMEM_LIMIT_BYTES),
        name=f"ffn_l{layer}_r{total}",
    )(x2, n2, wfi, wfo, fg)


def _ssm_params(lam_re, lam_im, log_dt, b_re, b_im, c_re, c_im):
    g, p, gc = SSM_GROUPS, SSM_STATE, SSM_GROUP_CH
    dt = jnp.exp(log_dt)[..., None]
    mag = jnp.exp(lam_re * dt)
    lbr, lbi = mag * jnp.cos(lam_im * dt), mag * jnp.sin(lam_im * dt)
    den = lam_re * lam_re + lam_im * lam_im
    qr = ((lbr - 1.0) * lam_re + lbi * lam_im) / den
    qi = (lbi * lam_re - (lbr - 1.0) * lam_im) / den
    bbr = qr[..., None] * b_re - qi[..., None] * b_im
    bbi = qr[..., None] * b_im + qi[..., None] * b_re
    eye = jnp.eye(g, dtype=f32)

    def to_bmat(b):
        bt = jnp.transpose(b, (0, 1, 3, 2))
        return (bt[:, :, :, None, :] * eye[None, :, None, :, None]).reshape(DEPTH, g * gc, g * p)

    def to_cmat(cm):
        ct = jnp.transpose(cm, (0, 1, 3, 2))
        return (ct[:, :, :, None, :] * eye[None, :, None, :, None]).reshape(DEPTH, g * p, g * gc)

    bmat = jnp.concatenate([to_bmat(bbr), to_bmat(bbi)], axis=-1).astype(bf16)
    cmat = jnp.concatenate([to_cmat(c_re), to_cmat(-c_im)], axis=-2).astype(bf16)
    lam = jnp.stack([lbr.reshape(DEPTH, g * p), lbi.reshape(DEPTH, g * p)], axis=1)
    return lam, bmat, cmat


def _pool_blockdiag(pool_w):
    k = len(POOL_WINDOWS)
    eye = jnp.eye(k, dtype=f32)
    w = pool_w[:, :, :, None, :] * eye[None, :, None, :, None]
    return w.reshape(DEPTH, MIX_W, MIX_W).astype(bf16)


def _time_major_state(s):
    return jnp.transpose(s, (0, 2, 1, 3))


def kernel(x_prompt, x_sample, state_pool, state_ssm_re, state_ssm_im, state_shortconv, state_conformer,
           norm1_g, norm2_g, final_g, w_in, b_gate, pool_w, pool_scale, lam_re, lam_im, log_dt,
           b_re, b_im, c_re, c_im, d_skip, w_glu, b_glu, sc_w, cf_w, cf_ln_g, cf_ln_b,
           w_branch, w_out, w_ffn_in, w_ffn_out):
    nb = x_prompt.shape[0]
    lam, bmat, cmat = _ssm_params(lam_re, lam_im, log_dt, b_re, b_im, c_re, c_im)
    row = lambda a: a.reshape(DEPTH, 1, a.shape[-1])
    taps = lambda w: jnp.broadcast_to(w[:, :, None, :], w.shape[:2] + (SUBLANES, MIX_W))
    mixer_w = (row(norm1_g), w_in.astype(bf16), b_gate, _pool_blockdiag(pool_w), row(pool_scale),
               lam, bmat, cmat, row(d_skip), w_glu.astype(bf16), row(b_glu), taps(sc_w), taps(cf_w),
               row(cf_ln_g), row(cf_ln_b), w_branch.astype(bf16), w_out.astype(bf16))
    n2 = row(norm2_g)
    wfi, wfo = w_ffn_in.astype(bf16), w_ffn_out.astype(bf16)
    fg = final_g.reshape(1, D_MODEL)

    groups = []
    zeros = lambda h: jnp.zeros((DEPTH, h, nb, MIX_W), f32)
    z_ssm = jnp.zeros((DEPTH, nb, SSM_W), f32)
    groups.append(dict(x=x_prompt, seq_major=True, pos0=0, tt=64, nsb=nb,
                       st=(zeros(POOL_BUF), z_ssm, z_ssm, zeros(SC_WIDTH - 1), zeros(CF_WIDTH - 1))))
    ns = x_sample.shape[0]
    groups.append(dict(x=jnp.transpose(x_sample, (1, 0, 2)), seq_major=False, pos0=PAST_LEN,
                       tt=x_sample.shape[1], nsb=64,
                       st=(_time_major_state(state_pool), state_ssm_re.reshape(DEPTH, ns, SSM_W),
                           state_ssm_im.reshape(DEPTH, ns, SSM_W), _time_major_state(state_shortconv),
                           _time_major_state(state_conformer))))

    results = []
    for grp in groups:
        x3, seq_major = grp["x"], grp["seq_major"]
        outs = [[] for _ in range(5)]
        for layer in range(DEPTH):
            last = layer == DEPTH - 1
            x3, po, reo, imo, sco, cfo = _mixer_call(layer, x3, grp["st"], mixer_w, tt=grp["tt"], nsb=grp["nsb"],
                                                     pos0=grp["pos0"], seq_major_in=seq_major and layer == 0)
            t_total, n_seq, _ = x3.shape
            for lst, o in zip(outs, (po, reo, imo, sco, cfo)):
                lst.append(o)
            x2 = _ffn_call(layer, x3.reshape(t_total * n_seq, D_MODEL), n2, wfi, wfo, fg, rows=FFN_ROWS,
                           final=last, seq_major_out=n_seq if (seq_major and last) else 0)
            x3 = x2 if (seq_major and last) else x2.reshape(t_total, n_seq, D_MODEL)
        y = x3 if seq_major else jnp.transpose(x3, (1, 0, 2))
        pool = jnp.transpose(jnp.stack(outs[0]), (0, 2, 1, 3))
        sre = jnp.stack(outs[1]).reshape(DEPTH, n_seq, SSM_GROUPS, SSM_STATE)
        sim = jnp.stack(outs[2]).reshape(DEPTH, n_seq, SSM_GROUPS, SSM_STATE)
        sc = jnp.transpose(jnp.stack(outs[3]), (0, 2, 1, 3))
        cf = jnp.transpose(jnp.stack(outs[4]), (0, 2, 1, 3))
        results.append((y, pool, sre, sim, sc, cf))

    (yp, pp, rp, ip, cp, fp), (ys, ps, rs, is_, cs, fs) = results
    return (yp, ys, pp, ps, rp, rs, ip, is_, cp, cs, fp, fs)
```

```python
import functools
import math

import jax
import jax.numpy as jnp
from jax import lax
from jax.experimental import pallas as pl
from jax.experimental.pallas import tpu as pltpu

D_MODEL = 1024
DEPTH = 4
PAST_LEN = 16384
N_BRANCH = 4
MIX_W = D_MODEL // 4
POOL_WINDOWS = (2, 4, 8, 16)
POOL_GW = MIX_W // len(POOL_WINDOWS)
POOL_BUF = max(POOL_WINDOWS) - 1
SSM_GROUP_CH = 16
SSM_GROUPS = MIX_W // SSM_GROUP_CH
SSM_STATE = 64
SSM_W = SSM_GROUPS * SSM_STATE
SC_WIDTH = 3
CF_WIDTH = 31
FF_DIM = -(-8 * D_MODEL // (3 * 256)) * 256
OFF_POOL = 0
OFF_SSM = OFF_POOL + MIX_W
OFF_SC = OFF_SSM + MIX_W
OFF_CF = OFF_SC + 3 * MIX_W
OFF_GATE = OFF_CF + 2 * MIX_W
IN_COLS = OFF_GATE + N_BRANCH * D_MODEL
RMS_EPS = 1e-6
LN_EPS = 1e-5

SUBLANES = 8
VMEM_LIMIT_BYTES = 56 * 1024 * 1024
FFN_COL_CHUNKS = ((0, 1024), (1024, 2048), (2048, FF_DIM))
FFN_ROWS = 512
W_STEPS = 8
LANES = 128
CONV_TIME_BLOCK = 2

f32 = jnp.float32
bf16 = jnp.bfloat16


def _sigmoid(x):
    return 0.5 * jnp.tanh(0.5 * x) + 0.5


def _gelu_tanh(x):
    c = math.sqrt(2.0 / math.pi)
    return x * (0.5 * (1.0 + jnp.tanh(c * (x + 0.044715 * (x * x * x)))))


def _rmsnorm(x, g):
    y = x * lax.rsqrt(jnp.mean(x * x, axis=-1, keepdims=True) + RMS_EPS)
    return y * g


def _dot(a, b):
    return jnp.dot(a, b, preferred_element_type=f32)


def _causal_conv(ext_s, w_ref, width, tt, nsb, dyn_zero):
    tb = min(tt, CONV_TIME_BLOCK)
    assert tt % tb == 0
    out0 = (width - 1 + tt) * nsb
    for l0 in range(0, MIX_W, LANES):
        for r0 in range(0, nsb, SUBLANES):
            for t0 in range(0, tt, tb):
                acc = [None] * tb
                for e in range(tb + width - 1):
                    row = pl.multiple_of((t0 + e) * nsb + r0 + dyn_zero, SUBLANES)
                    xe = ext_s[pl.ds(row, SUBLANES), l0:l0 + LANES]
                    for b in range(max(0, e - width + 1), min(tb, e + 1)):
                        term = w_ref[e - b, :, l0:l0 + LANES] * xe
                        acc[b] = term if acc[b] is None else acc[b] + term
                for b in range(tb):
                    row = pl.multiple_of(out0 + (t0 + b) * nsb + r0 + dyn_zero, SUBLANES)
                    ext_s[pl.ds(row, SUBLANES), l0:l0 + LANES] = acc[b]


def _mixer_kernel(zero_ref, x_ref, sp_ref, sre_ref, sim_ref, ssc_ref, scf_ref,
                  n1_ref, winc_ref, bgate_ref, wpool_ref, pscale_ref, lam_ref, bmat_ref, cmat_ref,
                  dskip_ref, wglu_ref, bglu_ref, scw_ref, cfw_ref, lng_ref, lnb_ref, wbrc_ref, woutc_ref,
                  xo_ref, po_ref, reo_ref, imo_ref, sco_ref, cfo_ref,
                  win_ref, wbr_ref, wout_ref, bu_s, extp_s, exts_s, extc_s, h_s, *maybe_xt_s,
                  tt, nsb, n_t, pos0, seq_major_in):
    step = pl.program_id(0)

    @pl.when(step < W_STEPS)
    def _cast_weights():
        for chunk_ref, dst in ((winc_ref, win_ref), (wbrc_ref, wbr_ref), (woutc_ref, wout_ref)):
            n = chunk_ref.shape[0]
            dst[pl.ds(pl.multiple_of(step * n, n), n), :] = chunk_ref[...].astype(bf16)

    @pl.when(step >= W_STEPS)
    def _tile():
        _mixer_tile(zero_ref, x_ref, sp_ref, sre_ref, sim_ref, ssc_ref, scf_ref,
                    n1_ref, win_ref, bgate_ref, wpool_ref, pscale_ref, lam_ref, bmat_ref, cmat_ref,
                    dskip_ref, wglu_ref, bglu_ref, scw_ref, cfw_ref, lng_ref, lnb_ref, wbr_ref, wout_ref,
                    xo_ref, po_ref, reo_ref, imo_ref, sco_ref, cfo_ref,
                    bu_s, extp_s, exts_s, extc_s, h_s, *maybe_xt_s,
                    t_idx=(step - W_STEPS) % n_t, tt=tt, nsb=nsb, pos0=pos0, seq_major_in=seq_major_in)


def _mixer_tile(zero_ref, x_ref, sp_ref, sre_ref, sim_ref, ssc_ref, scf_ref,
                n1_ref, win_ref, bgate_ref, wpool_ref, pscale_ref, lam_ref, bmat_ref, cmat_ref,
                dskip_ref, wglu_ref, bglu_ref, scw_ref, cfw_ref, lng_ref, lnb_ref, wbr_ref, wout_ref,
                xo_ref, po_ref, reo_ref, imo_ref, sco_ref, cfo_ref,
                bu_s, extp_s, exts_s, extc_s, h_s, *maybe_xt_s,
                t_idx, tt, nsb, pos0, seq_major_in):
    rows = tt * nsb
    hp, hs, hc = POOL_BUF * nsb, (SC_WIDTH - 1) * nsb, (CF_WIDTH - 1) * nsb
    dyn_zero = zero_ref[0]

    @pl.when(t_idx == 0)
    def _load_state():
        extp_s[0:hp, :] = sp_ref[...].reshape(hp, MIX_W)
        exts_s[0:hs, :] = ssc_ref[...].reshape(hs, MIX_W)
        extc_s[0:hc, :] = scf_ref[...].reshape(hc, MIX_W)
        h_s[:, 0:SSM_W] = sre_ref[...]
        h_s[:, SSM_W:] = sim_ref[...]

    if seq_major_in:
        (xt_s,) = maybe_xt_s
        for n in range(nsb):
            for j in range(D_MODEL // LANES):
                xt_s[j, pl.ds(n, tt, stride=nsb), :] = x_ref[n, :, j * LANES:(j + 1) * LANES]
        x = jnp.concatenate([xt_s[j] for j in range(D_MODEL // LANES)], axis=1)
    else:
        x = x_ref[...].reshape(rows, D_MODEL)
    hn = _rmsnorm(x, n1_ref[...]).astype(bf16)

    def proj(lo, hi):
        return _dot(hn, win_ref[:, lo:hi])

    def gate(k):
        lo = OFF_GATE + k * D_MODEL
        return _sigmoid(proj(lo, lo + D_MODEL) + bgate_ref[k:k + 1, :])

    vu = proj(OFF_POOL, OFF_SC)
    v, u = vu[:, 0:MIX_W], vu[:, MIX_W:]
    ab = proj(OFF_CF, OFF_GATE)
    bu_s[...] = _dot(u.astype(bf16), bmat_ref[...])
    scp = proj(OFF_SC, OFF_CF)
    gate_c = gate(2)
    gate_a = gate(0)

    extc_s[hc:hc + rows, :] = ab[:, 0:MIX_W] * _sigmoid(ab[:, MIX_W:])
    _causal_conv(extc_s, cfw_ref, CF_WIDTH, tt, nsb, dyn_zero)
    new_cf = extc_s[rows:rows + hc, :]
    extc_s[0:hc, :] = new_cf
    cfo_ref[...] = new_cf.reshape(CF_WIDTH - 1, nsb, MIX_W)

    exts_s[hs:hs + rows, :] = scp[:, MIX_W:2 * MIX_W] * scp[:, 2 * MIX_W:]
    _causal_conv(exts_s, scw_ref, SC_WIDTH, tt, nsb, dyn_zero)
    yc = scp[:, 0:MIX_W] * exts_s[hs + rows:hs + 2 * rows, :]
    new_sc = exts_s[rows:rows + hs, :]
    exts_s[0:hs, :] = new_sc
    sco_ref[...] = new_sc.reshape(SC_WIDTH - 1, nsb, MIX_W)

    extp_s[hp:hp + rows, :] = v
    ext = extp_s[...]
    s2 = ext[nsb:, :] + ext[:-nsb, :]
    s4 = s2[2 * nsb:, :] + s2[:-2 * nsb, :]
    s8 = s4[4 * nsb:, :] + s4[:-4 * nsb, :]
    s16 = s8[8 * nsb:, :] + s8[:-8 * nsb, :]
    lane = lax.broadcasted_iota(jnp.int32, (1, MIX_W), 1)
    grp = lane // POOL_GW
    wsum = jnp.where(grp == 0, s2[14 * nsb:, :],
                     jnp.where(grp == 1, s4[12 * nsb:, :],
                               jnp.where(grp == 2, s8[8 * nsb:, :], s16)))
    win = jnp.where(grp == 0, 2.0, jnp.where(grp == 1, 4.0, jnp.where(grp == 2, 8.0, 16.0))).astype(f32)
    step = (lax.broadcasted_iota(jnp.int32, (rows, MIX_W), 0) >> (nsb.bit_length() - 1)) + t_idx * tt
    pos = step.astype(f32) + float(pos0)
    cnt = jnp.minimum(pos + 1.0, win)
    d = wsum / cnt - v
    ya = _dot(d.astype(bf16), wpool_ref[...]) * pscale_ref[...]
    new_pool = extp_s[rows:rows + hp, :]
    extp_s[0:hp, :] = new_pool
    po_ref[...] = new_pool.reshape(POOL_BUF, nsb, MIX_W)

    merged = (gate_a * _dot(ya.astype(bf16), wbr_ref[0:MIX_W, :])
              + gate_c * _dot(yc.astype(bf16), wbr_ref[2 * MIX_W:3 * MIX_W, :]))
    gate_d = gate(3)

    lam_r = jnp.broadcast_to(lam_ref[0:1, :], (SUBLANES, SSM_W))
    lam_i = jnp.broadcast_to(lam_ref[1:2, :], (SUBLANES, SSM_W))
    for j in range(nsb // SUBLANES):
        r0 = j * SUBLANES
        hr, hi = h_s[r0:r0 + SUBLANES, 0:SSM_W], h_s[r0:r0 + SUBLANES, SSM_W:]
        for t in range(tt):
            row = t * nsb + r0
            nr = lam_r * hr - lam_i * hi + bu_s[row:row + SUBLANES, 0:SSM_W]
            ni = lam_r * hi + lam_i * hr + bu_s[row:row + SUBLANES, SSM_W:]
            bu_s[row:row + SUBLANES, 0:SSM_W] = nr
            bu_s[row:row + SUBLANES, SSM_W:] = ni
            hr, hi = nr, ni
        h_s[r0:r0 + SUBLANES, 0:SSM_W] = hr
        h_s[r0:r0 + SUBLANES, SSM_W:] = hi
    reo_ref[...] = h_s[:, 0:SSM_W]
    imo_ref[...] = h_s[:, SSM_W:]
    y = (_dot(bu_s[:, 0:SSM_W].astype(bf16), cmat_ref[0:SSM_W, :])
         + _dot(bu_s[:, SSM_W:].astype(bf16), cmat_ref[SSM_W:, :]) + dskip_ref[...] * u)

    conv = extc_s[hc + rows:hc + 2 * rows, :]
    mu = jnp.mean(conv, axis=-1, keepdims=True)
    cen = conv - mu
    var = jnp.mean(cen * cen, axis=-1, keepdims=True)
    yn = cen * lax.rsqrt(var + LN_EPS) * lng_ref[...] + lnb_ref[...]
    yd = yn * _sigmoid(yn)
    merged = merged + gate_d * _dot(yd.astype(bf16), wbr_ref[3 * MIX_W:, :])
    gate_b = gate(1)

    z = _gelu_tanh(y)
    yb = z * _sigmoid(_dot(z.astype(bf16), wglu_ref[...]) + bglu_ref[...])
    merged = merged + gate_b * _dot(yb.astype(bf16), wbr_ref[MIX_W:2 * MIX_W, :])

    xo = x + _dot(merged.astype(bf16), wout_ref[...])
    xo_ref[...] = xo.reshape(tt, nsb, D_MODEL)


def _const_spec(block_shape, index):
    return pl.BlockSpec(block_shape, lambda i: index, pipeline_mode=pl.Buffered(1))


def _chunk_spec(rows_total, cols, layer):
    return pl.BlockSpec((None, rows_total // W_STEPS, cols), lambda i: (layer, jnp.minimum(i, W_STEPS - 1), 0))


def _mixer_call(layer, x3, st, wts, *, tt, nsb, pos0, seq_major_in=False):
    if seq_major_in:
        n_seq, t_total, _ = x3.shape
    else:
        t_total, n_seq, _ = x3.shape
    assert nsb % SUBLANES == 0 and nsb & (nsb - 1) == 0 and n_seq % nsb == 0 and t_total % tt == 0
    rows = tt * nsb
    n_s, n_t = n_seq // nsb, t_total // tt
    sp, sre, sim, ssc, scf = st
    c = MIX_W

    def tile(i):
        q = jnp.maximum(i - W_STEPS, 0)
        return q // n_t, q % n_t

    def hist_spec(h):
        return pl.BlockSpec((None, h, nsb, c), lambda i: (layer, 0, tile(i)[0], 0))

    ssm_spec = pl.BlockSpec((None, nsb, SSM_W), lambda i: (layer, tile(i)[0], 0))
    x_spec = pl.BlockSpec((tt, nsb, D_MODEL), lambda i: (tile(i)[1], tile(i)[0], 0))
    x_in_spec = pl.BlockSpec((nsb, tt, D_MODEL), lambda i: (tile(i)[0], tile(i)[1], 0)) if seq_major_in else x_spec
    l3 = (layer, 0, 0)
    in_specs = [
        pl.BlockSpec(memory_space=pltpu.SMEM),
        x_in_spec, hist_spec(POOL_BUF), ssm_spec, ssm_spec, hist_spec(SC_WIDTH - 1), hist_spec(CF_WIDTH - 1),
        _const_spec((None, 1, D_MODEL), l3),
        _chunk_spec(D_MODEL, IN_COLS, layer),
        _const_spec((None, N_BRANCH, D_MODEL), l3),
        _const_spec((None, c, c), l3),
        _const_spec((None, 1, c), l3),
        _const_spec((None, 2, SSM_W), l3),
        _const_spec((None, c, 2 * SSM_W), l3),
        _const_spec((None, 2 * SSM_W, c), l3),
        _const_spec((None, 1, c), l3),
        _const_spec((None, c, c), l3),
        _const_spec((None, 1, c), l3),
        _const_spec((None, SC_WIDTH, SUBLANES, c), (layer, 0, 0, 0)),
        _const_spec((None, CF_WIDTH, SUBLANES, c), (layer, 0, 0, 0)),
        _const_spec((None, 1, c), l3),
        _const_spec((None, 1, c), l3),
        _chunk_spec(N_BRANCH * c, D_MODEL, layer),
        _chunk_spec(D_MODEL, D_MODEL, layer),
    ]

    def hist_out(h):
        return pl.BlockSpec((h, nsb, c), lambda i: (0, tile(i)[0], 0))

    ssm_out = pl.BlockSpec((nsb, SSM_W), lambda i: (tile(i)[0], 0))
    out_specs = [x_spec, hist_out(POOL_BUF), ssm_out, ssm_out, hist_out(SC_WIDTH - 1), hist_out(CF_WIDTH - 1)]
    out_shape = [
        jax.ShapeDtypeStruct((t_total, n_seq, D_MODEL), f32),
        jax.ShapeDtypeStruct((POOL_BUF, n_seq, c), f32),
        jax.ShapeDtypeStruct((n_seq, SSM_W), f32),
        jax.ShapeDtypeStruct((n_seq, SSM_W), f32),
        jax.ShapeDtypeStruct((SC_WIDTH - 1, n_seq, c), f32),
        jax.ShapeDtypeStruct((CF_WIDTH - 1, n_seq, c), f32),
    ]
    scratch = [
        pltpu.VMEM((D_MODEL, IN_COLS), bf16),
        pltpu.VMEM((N_BRANCH * c, D_MODEL), bf16),
        pltpu.VMEM((D_MODEL, D_MODEL), bf16),
        pltpu.VMEM((rows, 2 * SSM_W), f32),
        pltpu.VMEM((POOL_BUF * nsb + rows, c), f32),
        pltpu.VMEM(((SC_WIDTH - 1) * nsb + 2 * rows, c), f32),
        pltpu.VMEM(((CF_WIDTH - 1) * nsb + 2 * rows, c), f32),
        pltpu.VMEM((nsb, 2 * SSM_W), f32),
    ]
    if seq_major_in:
        scratch.append(pltpu.VMEM((D_MODEL // LANES, rows, LANES), f32))
    kern = functools.partial(_mixer_kernel, tt=tt, nsb=nsb, n_t=n_t, pos0=pos0, seq_major_in=seq_major_in)
    return pl.pallas_call(
        kern, grid=(W_STEPS + n_s * n_t,), in_specs=in_specs, out_specs=out_specs, out_shape=out_shape,
        scratch_shapes=scratch,
        compiler_params=pltpu.CompilerParams(dimension_semantics=("arbitrary",),
                                             vmem_limit_bytes=VMEM_LIMIT_BYTES),
        name=f"mixer_l{layer}_n{nsb}",
    )(jnp.zeros((1,), jnp.int32), x3, sp, sre, sim, ssc, scf, *wts)


def _ffn_kernel(x_ref, n2_ref, wfic_ref, wfoc_ref, fg_ref, o_ref, wfi_ref, wfo_ref, *maybe_t_s,
                final, seq_major_out):
    step = pl.program_id(0)

    @pl.when(step < W_STEPS)
    def _cast_weights():
        for chunk_ref, dst in ((wfic_ref, wfi_ref), (wfoc_ref, wfo_ref)):
            n = chunk_ref.shape[0]
            dst[pl.ds(pl.multiple_of(step * n, n), n), :] = chunk_ref[...].astype(bf16)

    @pl.when(step >= W_STEPS)
    def _tile():
        _ffn_tile(x_ref, n2_ref, wfi_ref, wfo_ref, fg_ref, o_ref, *maybe_t_s,
                  final=final, seq_major_out=seq_major_out)


def _ffn_tile(x_ref, n2_ref, wfi_ref, wfo_ref, fg_ref, o_ref, *maybe_t_s, final, seq_major_out):
    x = x_ref[...]
    hn = _rmsnorm(x, n2_ref[...]).astype(bf16)
    acc = x
    for lo, hi in FFN_COL_CHUNKS:
        g = _dot(hn, wfi_ref[:, lo:hi])
        up = _dot(hn, wfi_ref[:, FF_DIM + lo:FF_DIM + hi])
        act = (g * _sigmoid(g)) * up
        acc = acc + _dot(act.astype(bf16), wfo_ref[lo:hi, :])
    if final:
        acc = _rmsnorm(acc, fg_ref[...])
    if seq_major_out:
        (t_s,) = maybe_t_s
        n_seq, steps, _ = o_ref.shape
        for j in range(D_MODEL // LANES):
            t_s[j] = acc[:, j * LANES:(j + 1) * LANES]
        for n in range(n_seq):
            for j in range(D_MODEL // LANES):
                o_ref[n, :, j * LANES:(j + 1) * LANES] = t_s[j, pl.ds(n, steps, stride=n_seq), :]
    else:
        o_ref[...] = acc


def _ffn_call(layer, x2, n2, wfi, wfo, fg, *, rows, final, seq_major_out=0):
    total = x2.shape[0]
    tile = lambda i: jnp.maximum(i - W_STEPS, 0)
    x_spec = pl.BlockSpec((rows, D_MODEL), lambda i: (tile(i), 0))
    scratch = [pltpu.VMEM((D_MODEL, 2 * FF_DIM), bf16), pltpu.VMEM((FF_DIM, D_MODEL), bf16)]
    if seq_major_out:
        steps = rows // seq_major_out
        out_spec = pl.BlockSpec((seq_major_out, steps, D_MODEL), lambda i: (0, tile(i), 0))
        out_shape = jax.ShapeDtypeStruct((seq_major_out, total // seq_major_out, D_MODEL), f32)
        scratch.append(pltpu.VMEM((D_MODEL // LANES, rows, LANES), f32))
    else:
        out_spec, out_shape = x_spec, jax.ShapeDtypeStruct(x2.shape, f32)

    return pl.pallas_call(
        functools.partial(_ffn_kernel, final=final, seq_major_out=seq_major_out),
        grid=(W_STEPS + total // rows,),
        in_specs=[x_spec,
                  _const_spec((None, 1, D_MODEL), (layer, 0, 0)),
                  _chunk_spec(D_MODEL, 2 * FF_DIM, layer),
                  _chunk_spec(FF_DIM, D_MODEL, layer),
                  _const_spec((1, D_MODEL), (0, 0))],
        out_specs=out_spec,
        out_shape=out_shape,
        scratch_shapes=scratch,
        compiler_params=pltpu.CompilerParams(dimension_semantics=("arbitrary",),
                                             vmem_limit_bytes=VMEM_LIMIT_BYTES),
        name=f"ffn_l{layer}_r{total}",
    )(x2, n2, wfi, wfo, fg)


def _ssm_params(lam_re, lam_im, log_dt, b_re, b_im, c_re, c_im):
    g, p, gc = SSM_GROUPS, SSM_STATE, SSM_GROUP_CH
    dt = jnp.exp(log_dt)[..., None]
    mag = jnp.exp(lam_re * dt)
    lbr, lbi = mag * jnp.cos(lam_im * dt), mag * jnp.sin(lam_im * dt)
    den = lam_re * lam_re + lam_im * lam_im
    qr = ((lbr - 1.0) * lam_re + lbi * lam_im) / den
    qi = (lbi * lam_re - (lbr - 1.0) * lam_im) / den
    bbr = qr[..., None] * b_re - qi[..., None] * b_im
    bbi = qr[..., None] * b_im + qi[..., None] * b_re
    eye = jnp.eye(g, dtype=f32)

    def to_bmat(b):
        bt = jnp.transpose(b, (0, 1, 3, 2))
        return (bt[:, :, :, None, :] * eye[None, :, None, :, None]).reshape(DEPTH, g * gc, g * p)

    def to_cmat(cm):
        ct = jnp.transpose(cm, (0, 1, 3, 2))
        return (ct[:, :, :, None, :] * eye[None, :, None, :, None]).reshape(DEPTH, g * p, g * gc)

    bmat = jnp.concatenate([to_bmat(bbr), to_bmat(bbi)], axis=-1).astype(bf16)
    cmat = jnp.concatenate([to_cmat(c_re), to_cmat(-c_im)], axis=-2).astype(bf16)
    lam = jnp.stack([lbr.reshape(DEPTH, g * p), lbi.reshape(DEPTH, g * p)], axis=1)
    return lam, bmat, cmat


def _pool_blockdiag(pool_w):
    k = len(POOL_WINDOWS)
    eye = jnp.eye(k, dtype=f32)
    w = pool_w[:, :, :, None, :] * eye[None, :, None, :, None]
    return w.reshape(DEPTH, MIX_W, MIX_W).astype(bf16)


def _time_major_state(s):
    return jnp.transpose(s, (0, 2, 1, 3))


def kernel(x_prompt, x_sample, state_pool, state_ssm_re, state_ssm_im, state_shortconv, state_conformer,
           norm1_g, norm2_g, final_g, w_in, b_gate, pool_w, pool_scale, lam_re, lam_im, log_dt,
           b_re, b_im, c_re, c_im, d_skip, w_glu, b_glu, sc_w, cf_w, cf_ln_g, cf_ln_b,
           w_branch, w_out, w_ffn_in, w_ffn_out):
    nb = x_prompt.shape[0]
    lam, bmat, cmat = _ssm_params(lam_re, lam_im, log_dt, b_re, b_im, c_re, c_im)
    row = lambda a: a.reshape(DEPTH, 1, a.shape[-1])
    taps = lambda w: jnp.broadcast_to(w[:, :, None, :], w.shape[:2] + (SUBLANES, MIX_W))
    mixer_w = (row(norm1_g), w_in, b_gate, _pool_blockdiag(pool_w), row(pool_scale),
               lam, bmat, cmat, row(d_skip), w_glu.astype(bf16), row(b_glu), taps(sc_w), taps(cf_w),
               row(cf_ln_g), row(cf_ln_b), w_branch.reshape(DEPTH, N_BRANCH * MIX_W, D_MODEL), w_out)
    n2 = row(norm2_g)
    wfi, wfo = w_ffn_in, w_ffn_out
    fg = final_g.reshape(1, D_MODEL)

    groups = []
    zeros = lambda h: jnp.zeros((DEPTH, h, nb, MIX_W), f32)
    z_ssm = jnp.zeros((DEPTH, nb, SSM_W), f32)
    groups.append(dict(x=x_prompt, seq_major=True, pos0=0, tt=64, nsb=nb,
                       st=(zeros(POOL_BUF), z_ssm, z_ssm, zeros(SC_WIDTH - 1), zeros(CF_WIDTH - 1))))
    ns = x_sample.shape[0]
    groups.append(dict(x=jnp.transpose(x_sample, (1, 0, 2)), seq_major=False, pos0=PAST_LEN,
                       tt=x_sample.shape[1], nsb=32,
                       st=(_time_major_state(state_pool), state_ssm_re.reshape(DEPTH, ns, SSM_W),
                           state_ssm_im.reshape(DEPTH, ns, SSM_W), _time_major_state(state_shortconv),
                           _time_major_state(state_conformer))))

    results = []
    for grp in groups:
        x3, seq_major = grp["x"], grp["seq_major"]
        outs = [[] for _ in range(5)]
        for layer in range(DEPTH):
            last = layer == DEPTH - 1
            x3, po, reo, imo, sco, cfo = _mixer_call(layer, x3, grp["st"], mixer_w, tt=grp["tt"], nsb=grp["nsb"],
                                                     pos0=grp["pos0"], seq_major_in=seq_major and layer == 0)
            t_total, n_seq, _ = x3.shape
            for lst, o in zip(outs, (po, reo, imo, sco, cfo)):
                lst.append(o)
            x2 = _ffn_call(layer, x3.reshape(t_total * n_seq, D_MODEL), n2, wfi, wfo, fg, rows=FFN_ROWS,
                           final=last, seq_major_out=n_seq if (seq_major and last) else 0)
            x3 = x2 if (seq_major and last) else x2.reshape(t_total, n_seq, D_MODEL)
        y = x3 if seq_major else jnp.transpose(x3, (1, 0, 2))
        pool = jnp.transpose(jnp.stack(outs[0]), (0, 2, 1, 3))
        sre = jnp.stack(outs[1]).reshape(DEPTH, n_seq, SSM_GROUPS, SSM_STATE)
        sim = jnp.stack(outs[2]).reshape(DEPTH, n_seq, SSM_GROUPS, SSM_STATE)
        sc = jnp.transpose(jnp.stack(outs[3]), (0, 2, 1, 3))
        cf = jnp.transpose(jnp.stack(outs[4]), (0, 2, 1, 3))
        results.append((y, pool, sre, sim, sc, cf))

    (yp, pp, rp, ip, cp, fp), (ys, ps, rs, is_, cs, fs) = results
    return (yp, ys, pp, ps, rp, rs, ip, is_, cp, cs, fp, fs)
```

```python
import functools
import math

import jax
import jax.numpy as jnp
from jax import lax
from jax.experimental import pallas as pl
from jax.experimental.pallas import tpu as pltpu

D_MODEL = 1024
DEPTH = 4
PAST_LEN = 16384
N_BRANCH = 4
MIX_W = D_MODEL // 4
POOL_WINDOWS = (2, 4, 8, 16)
POOL_GW = MIX_W // len(POOL_WINDOWS)
POOL_BUF = max(POOL_WINDOWS) - 1
SSM_GROUP_CH = 16
SSM_GROUPS = MIX_W // SSM_GROUP_CH
SSM_STATE = 64
SSM_W = SSM_GROUPS * SSM_STATE
SC_WIDTH = 3
CF_WIDTH = 31
FF_DIM = -(-8 * D_MODEL // (3 * 256)) * 256
OFF_POOL = 0
OFF_SSM = OFF_POOL + MIX_W
OFF_SC = OFF_SSM + MIX_W
OFF_CF = OFF_SC + 3 * MIX_W
OFF_GATE = OFF_CF + 2 * MIX_W
IN_COLS = OFF_GATE + N_BRANCH * D_MODEL
RMS_EPS = 1e-6
LN_EPS = 1e-5

SUBLANES = 8
VMEM_LIMIT_BYTES = 56 * 1024 * 1024
FFN_COL_CHUNKS = ((0, 1024), (1024, 2048), (2048, FF_DIM))
FFN_ROWS = 512
W_STEPS = 8
LANES = 128
CONV_TIME_BLOCK = 2

f32 = jnp.float32
bf16 = jnp.bfloat16


def _sigmoid(x):
    return 0.5 * jnp.tanh(0.5 * x) + 0.5


def _gelu_tanh(x):
    c = math.sqrt(2.0 / math.pi)
    return x * (0.5 * (1.0 + jnp.tanh(c * (x + 0.044715 * (x * x * x)))))


def _rmsnorm(x, g):
    y = x * lax.rsqrt(jnp.mean(x * x, axis=-1, keepdims=True) + RMS_EPS)
    return y * g


def _dot(a, b):
    return jnp.dot(a, b, preferred_element_type=f32)


def _causal_conv(ext_s, w_ref, width, tt, nsb, dyn_zero):
    tb = min(tt, CONV_TIME_BLOCK)
    assert tt % tb == 0
    out0 = (width - 1 + tt) * nsb
    for l0 in range(0, MIX_W, LANES):
        for r0 in range(0, nsb, SUBLANES):
            for t0 in range(0, tt, tb):
                acc = [None] * tb
                for e in range(tb + width - 1):
                    row = pl.multiple_of((t0 + e) * nsb + r0 + dyn_zero, SUBLANES)
                    xe = ext_s[pl.ds(row, SUBLANES), l0:l0 + LANES]
                    for b in range(max(0, e - width + 1), min(tb, e + 1)):
                        term = w_ref[e - b, :, l0:l0 + LANES] * xe
                        acc[b] = term if acc[b] is None else acc[b] + term
                for b in range(tb):
                    row = pl.multiple_of(out0 + (t0 + b) * nsb + r0 + dyn_zero, SUBLANES)
                    ext_s[pl.ds(row, SUBLANES), l0:l0 + LANES] = acc[b]


def _mixer_kernel(zero_ref, x_ref, sp_ref, sre_ref, sim_ref, ssc_ref, scf_ref,
                  n1_ref, win_in, bgate_ref, wpool_ref, pscale_ref, lam_ref, bmat_ref, cmat_ref,
                  dskip_ref, wglu_ref, bglu_ref, scw_ref, cfw_ref, lng_ref, lnb_ref, wbr_in, wout_in,
                  xo_ref, po_ref, reo_ref, imo_ref, sco_ref, cfo_ref, *rest,
                  tt, nsb, n_t, pos0, seq_major_in, w_steps):
    step = pl.program_id(0)
    if w_steps:
        (win_o, wbr_o, wout_o, win_ref, wbr_ref, wout_ref, *scratch) = rest

        @pl.when(step < w_steps)
        def _cast_weights():
            for chunk_ref, dst, out in ((win_in, win_ref, win_o), (wbr_in, wbr_ref, wbr_o), (wout_in, wout_ref, wout_o)):
                n = chunk_ref.shape[0]
                chunk = chunk_ref[...].astype(bf16)
                dst[pl.ds(pl.multiple_of(step * n, n), n), :] = chunk
                out[...] = chunk
    else:
        win_ref, wbr_ref, wout_ref, scratch = win_in, wbr_in, wout_in, rest

    @pl.when(step >= w_steps)
    def _tile():
        _mixer_tile(zero_ref, x_ref, sp_ref, sre_ref, sim_ref, ssc_ref, scf_ref,
                    n1_ref, win_ref, bgate_ref, wpool_ref, pscale_ref, lam_ref, bmat_ref, cmat_ref,
                    dskip_ref, wglu_ref, bglu_ref, scw_ref, cfw_ref, lng_ref, lnb_ref, wbr_ref, wout_ref,
                    xo_ref, po_ref, reo_ref, imo_ref, sco_ref, cfo_ref, *scratch,
                    t_idx=(step - w_steps) % n_t, tt=tt, nsb=nsb, pos0=pos0, seq_major_in=seq_major_in)


def _mixer_tile(zero_ref, x_ref, sp_ref, sre_ref, sim_ref, ssc_ref, scf_ref,
                n1_ref, win_ref, bgate_ref, wpool_ref, pscale_ref, lam_ref, bmat_ref, cmat_ref,
                dskip_ref, wglu_ref, bglu_ref, scw_ref, cfw_ref, lng_ref, lnb_ref, wbr_ref, wout_ref,
                xo_ref, po_ref, reo_ref, imo_ref, sco_ref, cfo_ref,
                bu_s, extp_s, exts_s, extc_s, h_s, *maybe_xt_s,
                t_idx, tt, nsb, pos0, seq_major_in):
    rows = tt * nsb
    hp, hs, hc = POOL_BUF * nsb, (SC_WIDTH - 1) * nsb, (CF_WIDTH - 1) * nsb
    dyn_zero = zero_ref[0]

    @pl.when(t_idx == 0)
    def _load_state():
        extp_s[0:hp, :] = sp_ref[...].reshape(hp, MIX_W)
        exts_s[0:hs, :] = ssc_ref[...].reshape(hs, MIX_W)
        extc_s[0:hc, :] = scf_ref[...].reshape(hc, MIX_W)
        h_s[:, 0:SSM_W] = sre_ref[...]
        h_s[:, SSM_W:] = sim_ref[...]

    if seq_major_in:
        (xt_s,) = maybe_xt_s
        for n in range(nsb):
            for j in range(D_MODEL // LANES):
                xt_s[j, pl.ds(n, tt, stride=nsb), :] = x_ref[n, :, j * LANES:(j + 1) * LANES]
        x = jnp.concatenate([xt_s[j] for j in range(D_MODEL // LANES)], axis=1)
    else:
        x = x_ref[...].reshape(rows, D_MODEL)
    hn = _rmsnorm(x, n1_ref[...]).astype(bf16)

    def proj(lo, hi):
        return _dot(hn, win_ref[:, lo:hi])

    def gate(k):
        lo = OFF_GATE + k * D_MODEL
        return _sigmoid(proj(lo, lo + D_MODEL) + bgate_ref[k:k + 1, :])

    vu = proj(OFF_POOL, OFF_SC)
    v, u = vu[:, 0:MIX_W], vu[:, MIX_W:]
    ab = proj(OFF_CF, OFF_GATE)
    bu_s[...] = _dot(u.astype(bf16), bmat_ref[...])
    scp = proj(OFF_SC, OFF_CF)
    gate_c = gate(2)
    gate_a = gate(0)

    extc_s[hc:hc + rows, :] = ab[:, 0:MIX_W] * _sigmoid(ab[:, MIX_W:])
    _causal_conv(extc_s, cfw_ref, CF_WIDTH, tt, nsb, dyn_zero)
    new_cf = extc_s[rows:rows + hc, :]
    extc_s[0:hc, :] = new_cf
    cfo_ref[...] = new_cf.reshape(CF_WIDTH - 1, nsb, MIX_W)

    exts_s[hs:hs + rows, :] = scp[:, MIX_W:2 * MIX_W] * scp[:, 2 * MIX_W:]
    _causal_conv(exts_s, scw_ref, SC_WIDTH, tt, nsb, dyn_zero)
    yc = scp[:, 0:MIX_W] * exts_s[hs + rows:hs + 2 * rows, :]
    new_sc = exts_s[rows:rows + hs, :]
    exts_s[0:hs, :] = new_sc
    sco_ref[...] = new_sc.reshape(SC_WIDTH - 1, nsb, MIX_W)

    extp_s[hp:hp + rows, :] = v
    ext = extp_s[...]
    s2 = ext[nsb:, :] + ext[:-nsb, :]
    s4 = s2[2 * nsb:, :] + s2[:-2 * nsb, :]
    s8 = s4[4 * nsb:, :] + s4[:-4 * nsb, :]
    s16 = s8[8 * nsb:, :] + s8[:-8 * nsb, :]
    lane = lax.broadcasted_iota(jnp.int32, (1, MIX_W), 1)
    grp = lane // POOL_GW
    wsum = jnp.where(grp == 0, s2[14 * nsb:, :],
                     jnp.where(grp == 1, s4[12 * nsb:, :],
                               jnp.where(grp == 2, s8[8 * nsb:, :], s16)))
    win = jnp.where(grp == 0, 2.0, jnp.where(grp == 1, 4.0, jnp.where(grp == 2, 8.0, 16.0))).astype(f32)
    step = (lax.broadcasted_iota(jnp.int32, (rows, MIX_W), 0) >> (nsb.bit_length() - 1)) + t_idx * tt
    pos = step.astype(f32) + float(pos0)
    cnt = jnp.minimum(pos + 1.0, win)
    d = wsum / cnt - v
    ya = _dot(d.astype(bf16), wpool_ref[...]) * pscale_ref[...]
    new_pool = extp_s[rows:rows + hp, :]
    extp_s[0:hp, :] = new_pool
    po_ref[...] = new_pool.reshape(POOL_BUF, nsb, MIX_W)

    merged = (gate_a * _dot(ya.astype(bf16), wbr_ref[0:MIX_W, :])
              + gate_c * _dot(yc.astype(bf16), wbr_ref[2 * MIX_W:3 * MIX_W, :]))
    gate_d = gate(3)

    lam_r = jnp.broadcast_to(lam_ref[0:1, :], (SUBLANES, SSM_W))
    lam_i = jnp.broadcast_to(lam_ref[1:2, :], (SUBLANES, SSM_W))
    for j in range(nsb // SUBLANES):
        r0 = j * SUBLANES
        hr, hi = h_s[r0:r0 + SUBLANES, 0:SSM_W], h_s[r0:r0 + SUBLANES, SSM_W:]
        for t in range(tt):
            row = t * nsb + r0
            nr = lam_r * hr - lam_i * hi + bu_s[row:row + SUBLANES, 0:SSM_W]
            ni = lam_r * hi + lam_i * hr + bu_s[row:row + SUBLANES, SSM_W:]
            bu_s[row:row + SUBLANES, 0:SSM_W] = nr
            bu_s[row:row + SUBLANES, SSM_W:] = ni
            hr, hi = nr, ni
        h_s[r0:r0 + SUBLANES, 0:SSM_W] = hr
        h_s[r0:r0 + SUBLANES, SSM_W:] = hi
    reo_ref[...] = h_s[:, 0:SSM_W]
    imo_ref[...] = h_s[:, SSM_W:]
    y = (_dot(bu_s[:, 0:SSM_W].astype(bf16), cmat_ref[0:SSM_W, :])
         + _dot(bu_s[:, SSM_W:].astype(bf16), cmat_ref[SSM_W:, :]) + dskip_ref[...] * u)

    conv = extc_s[hc + rows:hc + 2 * rows, :]
    mu = jnp.mean(conv, axis=-1, keepdims=True)
    cen = conv - mu
    var = jnp.mean(cen * cen, axis=-1, keepdims=True)
    yn = cen * lax.rsqrt(var + LN_EPS) * lng_ref[...] + lnb_ref[...]
    yd = yn * _sigmoid(yn)
    merged = merged + gate_d * _dot(yd.astype(bf16), wbr_ref[3 * MIX_W:, :])
    gate_b = gate(1)

    z = _gelu_tanh(y)
    yb = z * _sigmoid(_dot(z.astype(bf16), wglu_ref[...]) + bglu_ref[...])
    merged = merged + gate_b * _dot(yb.astype(bf16), wbr_ref[MIX_W:2 * MIX_W, :])

    xo = x + _dot(merged.astype(bf16), wout_ref[...])
    xo_ref[...] = xo.reshape(tt, nsb, D_MODEL)


def _const_spec(block_shape, index):
    return pl.BlockSpec(block_shape, lambda i: index, pipeline_mode=pl.Buffered(1))


def _chunk_spec(rows_total, cols, layer):
    return pl.BlockSpec((None, rows_total // W_STEPS, cols), lambda i: (layer, jnp.minimum(i, W_STEPS - 1), 0))


def _chunk_out(rows_total, cols):
    return (pl.BlockSpec((rows_total // W_STEPS, cols), lambda i: (jnp.minimum(i, W_STEPS - 1), 0)),
            jax.ShapeDtypeStruct((rows_total, cols), bf16))


def _whole(shape):
    return pl.BlockSpec(shape, lambda i: (0,) * len(shape), pipeline_mode=pl.Buffered(1))


def _mixer_call(layer, x3, st, wts, *, tt, nsb, pos0, seq_major_in=False, bf16_weights=None):
    if seq_major_in:
        n_seq, t_total, _ = x3.shape
    else:
        t_total, n_seq, _ = x3.shape
    assert nsb % SUBLANES == 0 and nsb & (nsb - 1) == 0 and n_seq % nsb == 0 and t_total % tt == 0
    rows = tt * nsb
    n_s, n_t = n_seq // nsb, t_total // tt
    sp, sre, sim, ssc, scf = st
    c = MIX_W

    w_steps = W_STEPS if bf16_weights is None else 0

    def tile(i):
        q = jnp.maximum(i - w_steps, 0)
        return q // n_t, q % n_t

    def hist_spec(h):
        return pl.BlockSpec((None, h, nsb, c), lambda i: (layer, 0, tile(i)[0], 0))

    ssm_spec = pl.BlockSpec((None, nsb, SSM_W), lambda i: (layer, tile(i)[0], 0))
    x_spec = pl.BlockSpec((tt, nsb, D_MODEL), lambda i: (tile(i)[1], tile(i)[0], 0))
    x_in_spec = pl.BlockSpec((nsb, tt, D_MODEL), lambda i: (tile(i)[0], tile(i)[1], 0)) if seq_major_in else x_spec
    l3 = (layer, 0, 0)
    in_specs = [
        pl.BlockSpec(memory_space=pltpu.SMEM),
        x_in_spec, hist_spec(POOL_BUF), ssm_spec, ssm_spec, hist_spec(SC_WIDTH - 1), hist_spec(CF_WIDTH - 1),
        _const_spec((None, 1, D_MODEL), l3),
        _chunk_spec(D_MODEL, IN_COLS, layer) if w_steps else _whole((D_MODEL, IN_COLS)),
        _const_spec((None, N_BRANCH, D_MODEL), l3),
        _const_spec((None, c, c), l3),
        _const_spec((None, 1, c), l3),
        _const_spec((None, 2, SSM_W), l3),
        _const_spec((None, c, 2 * SSM_W), l3),
        _const_spec((None, 2 * SSM_W, c), l3),
        _const_spec((None, 1, c), l3),
        _const_spec((None, c, c), l3),
        _const_spec((None, 1, c), l3),
        _const_spec((None, SC_WIDTH, SUBLANES, c), (layer, 0, 0, 0)),
        _const_spec((None, CF_WIDTH, SUBLANES, c), (layer, 0, 0, 0)),
        _const_spec((None, 1, c), l3),
        _const_spec((None, 1, c), l3),
        _chunk_spec(N_BRANCH * c, D_MODEL, layer) if w_steps else _whole((N_BRANCH * c, D_MODEL)),
        _chunk_spec(D_MODEL, D_MODEL, layer) if w_steps else _whole((D_MODEL, D_MODEL)),
    ]

    def hist_out(h):
        return pl.BlockSpec((h, nsb, c), lambda i: (0, tile(i)[0], 0))

    ssm_out = pl.BlockSpec((nsb, SSM_W), lambda i: (tile(i)[0], 0))
    out_specs = [x_spec, hist_out(POOL_BUF), ssm_out, ssm_out, hist_out(SC_WIDTH - 1), hist_out(CF_WIDTH - 1)]
    out_shape = [
        jax.ShapeDtypeStruct((t_total, n_seq, D_MODEL), f32),
        jax.ShapeDtypeStruct((POOL_BUF, n_seq, c), f32),
        jax.ShapeDtypeStruct((n_seq, SSM_W), f32),
        jax.ShapeDtypeStruct((n_seq, SSM_W), f32),
        jax.ShapeDtypeStruct((SC_WIDTH - 1, n_seq, c), f32),
        jax.ShapeDtypeStruct((CF_WIDTH - 1, n_seq, c), f32),
    ]
    w_shapes = ((D_MODEL, IN_COLS), (N_BRANCH * c, D_MODEL), (D_MODEL, D_MODEL))
    scratch = []
    if w_steps:
        for spec, shape in (_chunk_out(*ws) for ws in w_shapes):
            out_specs.append(spec)
            out_shape.append(shape)
        scratch = [pltpu.VMEM(ws, bf16) for ws in w_shapes]
    scratch += [
        pltpu.VMEM((rows, 2 * SSM_W), f32),
        pltpu.VMEM((POOL_BUF * nsb + rows, c), f32),
        pltpu.VMEM(((SC_WIDTH - 1) * nsb + 2 * rows, c), f32),
        pltpu.VMEM(((CF_WIDTH - 1) * nsb + 2 * rows, c), f32),
        pltpu.VMEM((nsb, 2 * SSM_W), f32),
    ]
    if seq_major_in:
        scratch.append(pltpu.VMEM((D_MODEL // LANES, rows, LANES), f32))
    kern = functools.partial(_mixer_kernel, tt=tt, nsb=nsb, n_t=n_t, pos0=pos0, seq_major_in=seq_major_in,
                             w_steps=w_steps)
    if bf16_weights is not None:
        wts = list(wts)
        wts[1], wts[15], wts[16] = bf16_weights
    return pl.pallas_call(
        kern, grid=(w_steps + n_s * n_t,), in_specs=in_specs, out_specs=out_specs, out_shape=out_shape,
        scratch_shapes=scratch,
        compiler_params=pltpu.CompilerParams(dimension_semantics=("arbitrary",),
                                             vmem_limit_bytes=VMEM_LIMIT_BYTES),
        name=f"mixer_l{layer}_n{nsb}",
    )(jnp.zeros((1,), jnp.int32), x3, sp, sre, sim, ssc, scf, *wts)


def _ffn_kernel(x_ref, n2_ref, wfi_in, wfo_in, fg_ref, o_ref, *rest, final, seq_major_out, w_steps):
    step = pl.program_id(0)
    if w_steps:
        (wfi_o, wfo_o, wfi_ref, wfo_ref, *maybe_t_s) = rest

        @pl.when(step < w_steps)
        def _cast_weights():
            for chunk_ref, dst, out in ((wfi_in, wfi_ref, wfi_o), (wfo_in, wfo_ref, wfo_o)):
                n = chunk_ref.shape[0]
                chunk = chunk_ref[...].astype(bf16)
                dst[pl.ds(pl.multiple_of(step * n, n), n), :] = chunk
                out[...] = chunk
    else:
        wfi_ref, wfo_ref, maybe_t_s = wfi_in, wfo_in, rest

    @pl.when(step >= w_steps)
    def _tile():
        _ffn_tile(x_ref, n2_ref, wfi_ref, wfo_ref, fg_ref, o_ref, *maybe_t_s,
                  final=final, seq_major_out=seq_major_out)


def _ffn_tile(x_ref, n2_ref, wfi_ref, wfo_ref, fg_ref, o_ref, *maybe_t_s, final, seq_major_out):
    x = x_ref[...]
    hn = _rmsnorm(x, n2_ref[...]).astype(bf16)
    acc = x
    for lo, hi in FFN_COL_CHUNKS:
        g = _dot(hn, wfi_ref[:, lo:hi])
        up = _dot(hn, wfi_ref[:, FF_DIM + lo:FF_DIM + hi])
        act = (g * _sigmoid(g)) * up
        acc = acc + _dot(act.astype(bf16), wfo_ref[lo:hi, :])
    if final:
        acc = _rmsnorm(acc, fg_ref[...])
    if seq_major_out:
        (t_s,) = maybe_t_s
        n_seq, steps, _ = o_ref.shape
        for j in range(D_MODEL // LANES):
            t_s[j] = acc[:, j * LANES:(j + 1) * LANES]
        for n in range(n_seq):
            for j in range(D_MODEL // LANES):
                o_ref[n, :, j * LANES:(j + 1) * LANES] = t_s[j, pl.ds(n, steps, stride=n_seq), :]
    else:
        o_ref[...] = acc


def _ffn_call(layer, x2, n2, wfi, wfo, fg, *, rows, final, seq_major_out=0, bf16_weights=None):
    total = x2.shape[0]
    w_steps = W_STEPS if bf16_weights is None else 0
    tile = lambda i: jnp.maximum(i - w_steps, 0)
    x_spec = pl.BlockSpec((rows, D_MODEL), lambda i: (tile(i), 0))
    w_shapes = ((D_MODEL, 2 * FF_DIM), (FF_DIM, D_MODEL))
    if seq_major_out:
        steps = rows // seq_major_out
        out_specs = [pl.BlockSpec((seq_major_out, steps, D_MODEL), lambda i: (0, tile(i), 0))]
        out_shape = [jax.ShapeDtypeStruct((seq_major_out, total // seq_major_out, D_MODEL), f32)]
    else:
        out_specs, out_shape = [x_spec], [jax.ShapeDtypeStruct(x2.shape, f32)]
    scratch = []
    if w_steps:
        for spec, shape in (_chunk_out(*ws) for ws in w_shapes):
            out_specs.append(spec)
            out_shape.append(shape)
        scratch = [pltpu.VMEM(ws, bf16) for ws in w_shapes]
        w_specs = [_chunk_spec(D_MODEL, 2 * FF_DIM, layer), _chunk_spec(FF_DIM, D_MODEL, layer)]
    else:
        wfi, wfo = bf16_weights
        w_specs = [_whole(ws) for ws in w_shapes]
    if seq_major_out:
        scratch.append(pltpu.VMEM((D_MODEL // LANES, rows, LANES), f32))

    outs = pl.pallas_call(
        functools.partial(_ffn_kernel, final=final, seq_major_out=seq_major_out, w_steps=w_steps),
        grid=(w_steps + total // rows,),
        in_specs=[x_spec, _const_spec((None, 1, D_MODEL), (layer, 0, 0)), *w_specs, _const_spec((1, D_MODEL), (0, 0))],
        out_specs=out_specs,
        out_shape=out_shape,
        scratch_shapes=scratch,
        compiler_params=pltpu.CompilerParams(dimension_semantics=("arbitrary",),
                                             vmem_limit_bytes=VMEM_LIMIT_BYTES),
        name=f"ffn_l{layer}_r{total}",
    )(x2, n2, wfi, wfo, fg)
    return outs[0], (tuple(outs[1:]) if w_steps else None)


def _ssm_params(lam_re, lam_im, log_dt, b_re, b_im, c_re, c_im):
    g, p, gc = SSM_GROUPS, SSM_STATE, SSM_GROUP_CH
    dt = jnp.exp(log_dt)[..., None]
    mag = jnp.exp(lam_re * dt)
    lbr, lbi = mag * jnp.cos(lam_im * dt), mag * jnp.sin(lam_im * dt)
    den = lam_re * lam_re + lam_im * lam_im
    qr = ((lbr - 1.0) * lam_re + lbi * lam_im) / den
    qi = (lbi * lam_re - (lbr - 1.0) * lam_im) / den
    bbr = qr[..., None] * b_re - qi[..., None] * b_im
    bbi = qr[..., None] * b_im + qi[..., None] * b_re
    eye = jnp.eye(g, dtype=f32)

    def to_bmat(b):
        bt = jnp.transpose(b, (0, 1, 3, 2))
        return (bt[:, :, :, None, :] * eye[None, :, None, :, None]).reshape(DEPTH, g * gc, g * p)

    def to_cmat(cm):
        ct = jnp.transpose(cm, (0, 1, 3, 2))
        return (ct[:, :, :, None, :] * eye[None, :, None, :, None]).reshape(DEPTH, g * p, g * gc)

    bmat = jnp.concatenate([to_bmat(bbr), to_bmat(bbi)], axis=-1).astype(bf16)
    cmat = jnp.concatenate([to_cmat(c_re), to_cmat(-c_im)], axis=-2).astype(bf16)
    lam = jnp.stack([lbr.reshape(DEPTH, g * p), lbi.reshape(DEPTH, g * p)], axis=1)
    return lam, bmat, cmat


def _pool_blockdiag(pool_w):
    k = len(POOL_WINDOWS)
    eye = jnp.eye(k, dtype=f32)
    w = pool_w[:, :, :, None, :] * eye[None, :, None, :, None]
    return w.reshape(DEPTH, MIX_W, MIX_W).astype(bf16)


def _time_major_state(s):
    return jnp.transpose(s, (0, 2, 1, 3))


def kernel(x_prompt, x_sample, state_pool, state_ssm_re, state_ssm_im, state_shortconv, state_conformer,
           norm1_g, norm2_g, final_g, w_in, b_gate, pool_w, pool_scale, lam_re, lam_im, log_dt,
           b_re, b_im, c_re, c_im, d_skip, w_glu, b_glu, sc_w, cf_w, cf_ln_g, cf_ln_b,
           w_branch, w_out, w_ffn_in, w_ffn_out):
    nb = x_prompt.shape[0]
    lam, bmat, cmat = _ssm_params(lam_re, lam_im, log_dt, b_re, b_im, c_re, c_im)
    row = lambda a: a.reshape(DEPTH, 1, a.shape[-1])
    taps = lambda w: jnp.broadcast_to(w[:, :, None, :], w.shape[:2] + (SUBLANES, MIX_W))
    mixer_w = (row(norm1_g), w_in, b_gate, _pool_blockdiag(pool_w), row(pool_scale),
               lam, bmat, cmat, row(d_skip), w_glu.astype(bf16), row(b_glu), taps(sc_w), taps(cf_w),
               row(cf_ln_g), row(cf_ln_b), w_branch.reshape(DEPTH, N_BRANCH * MIX_W, D_MODEL), w_out)
    n2 = row(norm2_g)
    wfi, wfo = w_ffn_in, w_ffn_out
    fg = final_g.reshape(1, D_MODEL)

    groups = []
    zeros = lambda h: jnp.zeros((DEPTH, h, nb, MIX_W), f32)
    z_ssm = jnp.zeros((DEPTH, nb, SSM_W), f32)
    groups.append(dict(x=x_prompt, seq_major=True, pos0=0, tt=64, nsb=nb,
                       st=(zeros(POOL_BUF), z_ssm, z_ssm, zeros(SC_WIDTH - 1), zeros(CF_WIDTH - 1))))
    ns = x_sample.shape[0]
    groups.append(dict(x=jnp.transpose(x_sample, (1, 0, 2)), seq_major=False, pos0=PAST_LEN,
                       tt=x_sample.shape[1], nsb=64,
                       st=(_time_major_state(state_pool), state_ssm_re.reshape(DEPTH, ns, SSM_W),
                           state_ssm_im.reshape(DEPTH, ns, SSM_W), _time_major_state(state_shortconv),
                           _time_major_state(state_conformer))))

    results = []
    cast_mixer, cast_ffn = [None] * DEPTH, [None] * DEPTH
    for grp in groups:
        x3, seq_major = grp["x"], grp["seq_major"]
        outs = [[] for _ in range(5)]
        for layer in range(DEPTH):
            last = layer == DEPTH - 1
            res = _mixer_call(layer, x3, grp["st"], mixer_w, tt=grp["tt"], nsb=grp["nsb"], pos0=grp["pos0"],
                              seq_major_in=seq_major and layer == 0, bf16_weights=cast_mixer[layer])
            x3, po, reo, imo, sco, cfo = res[:6]
            if cast_mixer[layer] is None:
                cast_mixer[layer] = tuple(res[6:])
            t_total, n_seq, _ = x3.shape
            for lst, o in zip(outs, (po, reo, imo, sco, cfo)):
                lst.append(o)
            x2, cast = _ffn_call(layer, x3.reshape(t_total * n_seq, D_MODEL), n2, wfi, wfo, fg, rows=FFN_ROWS,
                                 final=last, seq_major_out=n_seq if (seq_major and last) else 0,
                                 bf16_weights=cast_ffn[layer])
            if cast_ffn[layer] is None:
                cast_ffn[layer] = cast
            x3 = x2 if (seq_major and last) else x2.reshape(t_total, n_seq, D_MODEL)
        y = x3 if seq_major else jnp.transpose(x3, (1, 0, 2))
        pool = jnp.transpose(jnp.stack(outs[0]), (0, 2, 1, 3))
        sre = jnp.stack(outs[1]).reshape(DEPTH, n_seq, SSM_GROUPS, SSM_STATE)
        sim = jnp.stack(outs[2]).reshape(DEPTH, n_seq, SSM_GROUPS, SSM_STATE)
        sc = jnp.transpose(jnp.stack(outs[3]), (0, 2, 1, 3))
        cf = jnp.transpose(jnp.stack(outs[4]), (0, 2, 1, 3))
        results.append((y, pool, sre, sim, sc, cf))

    (yp, pp, rp, ip, cp, fp), (ys, ps, rs, is_, cs, fs) = results
    return (yp, ys, pp, ps, rp, rs, ip, is_, cp, cs, fp, fs)
```

```python
import functools
import math

import jax
import jax.numpy as jnp
from jax import lax
from jax.experimental import pallas as pl
from jax.experimental.pallas import tpu as pltpu

D_MODEL = 1024
DEPTH = 4
PAST_LEN = 16384
N_BRANCH = 4
MIX_W = D_MODEL // 4
POOL_WINDOWS = (2, 4, 8, 16)
POOL_GW = MIX_W // len(POOL_WINDOWS)
POOL_BUF = max(POOL_WINDOWS) - 1
SSM_GROUP_CH = 16
SSM_GROUPS = MIX_W // SSM_GROUP_CH
SSM_STATE = 64
SSM_W = SSM_GROUPS * SSM_STATE
SC_WIDTH = 3
CF_WIDTH = 31
FF_DIM = -(-8 * D_MODEL // (3 * 256)) * 256
OFF_POOL = 0
OFF_SSM = OFF_POOL + MIX_W
OFF_SC = OFF_SSM + MIX_W
OFF_CF = OFF_SC + 3 * MIX_W
OFF_GATE = OFF_CF + 2 * MIX_W
IN_COLS = OFF_GATE + N_BRANCH * D_MODEL
RMS_EPS = 1e-6
LN_EPS = 1e-5

SUBLANES = 8
VMEM_LIMIT_BYTES = 56 * 1024 * 1024
FFN_COL_CHUNKS = ((0, 1024), (1024, 2048), (2048, FF_DIM))
FFN_ROWS = 512
W_STEPS = 8
LANES = 128
CONV_TIME_BLOCK = 2

f32 = jnp.float32
bf16 = jnp.bfloat16


def _sigmoid(x):
    return 0.5 * jnp.tanh(0.5 * x) + 0.5


def _gelu_tanh(x):
    c = math.sqrt(2.0 / math.pi)
    return x * (0.5 * (1.0 + jnp.tanh(c * (x + 0.044715 * (x * x * x)))))


def _rmsnorm(x, g):
    y = x * lax.rsqrt(jnp.mean(x * x, axis=-1, keepdims=True) + RMS_EPS)
    return y * g


def _dot(a, b):
    return jnp.dot(a, b, preferred_element_type=f32)


def _causal_conv(ext_s, w_ref, width, tt, nsb, dyn_zero):
    tb = min(tt, CONV_TIME_BLOCK)
    assert tt % tb == 0
    out0 = (width - 1 + tt) * nsb
    for l0 in range(0, MIX_W, LANES):
        for r0 in range(0, nsb, SUBLANES):
            for t0 in range(0, tt, tb):
                acc = [None] * tb
                for e in range(tb + width - 1):
                    row = pl.multiple_of((t0 + e) * nsb + r0 + dyn_zero, SUBLANES)
                    xe = ext_s[pl.ds(row, SUBLANES), l0:l0 + LANES]
                    for b in range(max(0, e - width + 1), min(tb, e + 1)):
                        term = w_ref[e - b, :, l0:l0 + LANES] * xe
                        acc[b] = term if acc[b] is None else acc[b] + term
                for b in range(tb):
                    row = pl.multiple_of(out0 + (t0 + b) * nsb + r0 + dyn_zero, SUBLANES)
                    ext_s[pl.ds(row, SUBLANES), l0:l0 + LANES] = acc[b]


def _mixer_kernel(zero_ref, x_ref, *refs, tt, nsb, n_t, pos0, seq_major_in, w_steps, has_state):
    state_refs, refs = (refs[:5], refs[5:]) if has_state else ((None,) * 5, refs)
    sp_ref, sre_ref, sim_ref, ssc_ref, scf_ref = state_refs
    (n1_ref, win_in, bgate_ref, wpool_ref, pscale_ref, lam_ref, bmat_ref, cmat_ref,
     dskip_ref, wglu_ref, bglu_ref, scw_ref, cfw_ref, lng_ref, lnb_ref, wbr_in, wout_in,
     xo_ref, po_ref, reo_ref, imo_ref, sco_ref, cfo_ref, *rest) = refs
    step = pl.program_id(0)
    if w_steps:
        (win_o, wbr_o, wout_o, win_ref, wbr_ref, wout_ref, *scratch) = rest

        @pl.when(step < w_steps)
        def _cast_weights():
            for chunk_ref, dst, out in ((win_in, win_ref, win_o), (wbr_in, wbr_ref, wbr_o), (wout_in, wout_ref, wout_o)):
                n = chunk_ref.shape[0]
                chunk = chunk_ref[...].astype(bf16)
                dst[pl.ds(pl.multiple_of(step * n, n), n), :] = chunk
                out[...] = chunk
    else:
        win_ref, wbr_ref, wout_ref, scratch = win_in, wbr_in, wout_in, rest

    @pl.when(step >= w_steps)
    def _tile():
        _mixer_tile(zero_ref, x_ref, sp_ref, sre_ref, sim_ref, ssc_ref, scf_ref,
                    n1_ref, win_ref, bgate_ref, wpool_ref, pscale_ref, lam_ref, bmat_ref, cmat_ref,
                    dskip_ref, wglu_ref, bglu_ref, scw_ref, cfw_ref, lng_ref, lnb_ref, wbr_ref, wout_ref,
                    xo_ref, po_ref, reo_ref, imo_ref, sco_ref, cfo_ref, *scratch,
                    t_idx=(step - w_steps) % n_t, tt=tt, nsb=nsb, pos0=pos0, seq_major_in=seq_major_in)


def _mixer_tile(zero_ref, x_ref, sp_ref, sre_ref, sim_ref, ssc_ref, scf_ref,
                n1_ref, win_ref, bgate_ref, wpool_ref, pscale_ref, lam_ref, bmat_ref, cmat_ref,
                dskip_ref, wglu_ref, bglu_ref, scw_ref, cfw_ref, lng_ref, lnb_ref, wbr_ref, wout_ref,
                xo_ref, po_ref, reo_ref, imo_ref, sco_ref, cfo_ref,
                bu_s, extp_s, exts_s, extc_s, h_s, *maybe_xt_s,
                t_idx, tt, nsb, pos0, seq_major_in):
    rows = tt * nsb
    hp, hs, hc = POOL_BUF * nsb, (SC_WIDTH - 1) * nsb, (CF_WIDTH - 1) * nsb
    dyn_zero = zero_ref[0]

    @pl.when(t_idx == 0)
    def _load_state():
        if sp_ref is None:
            extp_s[0:hp, :] = jnp.zeros((hp, MIX_W), f32)
            exts_s[0:hs, :] = jnp.zeros((hs, MIX_W), f32)
            extc_s[0:hc, :] = jnp.zeros((hc, MIX_W), f32)
            h_s[...] = jnp.zeros(h_s.shape, f32)
        else:
            extp_s[0:hp, :] = sp_ref[...].reshape(hp, MIX_W)
            exts_s[0:hs, :] = ssc_ref[...].reshape(hs, MIX_W)
            extc_s[0:hc, :] = scf_ref[...].reshape(hc, MIX_W)
            h_s[:, 0:SSM_W] = sre_ref[...]
            h_s[:, SSM_W:] = sim_ref[...]

    if seq_major_in:
        (xt_s,) = maybe_xt_s
        for n in range(nsb):
            for j in range(D_MODEL // LANES):
                xt_s[j, pl.ds(n, tt, stride=nsb), :] = x_ref[n, :, j * LANES:(j + 1) * LANES]
        x = jnp.concatenate([xt_s[j] for j in range(D_MODEL // LANES)], axis=1)
    else:
        x = x_ref[...].reshape(rows, D_MODEL)
    hn = _rmsnorm(x, n1_ref[...]).astype(bf16)

    def proj(lo, hi):
        return _dot(hn, win_ref[:, lo:hi])

    def gate(k):
        lo = OFF_GATE + k * D_MODEL
        return _sigmoid(proj(lo, lo + D_MODEL) + bgate_ref[k:k + 1, :])

    vu = proj(OFF_POOL, OFF_SC)
    v, u = vu[:, 0:MIX_W], vu[:, MIX_W:]
    ab = proj(OFF_CF, OFF_GATE)
    bu_s[...] = _dot(u.astype(bf16), bmat_ref[...])
    scp = proj(OFF_SC, OFF_CF)
    gate_c = gate(2)
    gate_a = gate(0)

    extc_s[hc:hc + rows, :] = ab[:, 0:MIX_W] * _sigmoid(ab[:, MIX_W:])
    _causal_conv(extc_s, cfw_ref, CF_WIDTH, tt, nsb, dyn_zero)
    new_cf = extc_s[rows:rows + hc, :]
    extc_s[0:hc, :] = new_cf
    cfo_ref[...] = new_cf.reshape(CF_WIDTH - 1, nsb, MIX_W)

    exts_s[hs:hs + rows, :] = scp[:, MIX_W:2 * MIX_W] * scp[:, 2 * MIX_W:]
    _causal_conv(exts_s, scw_ref, SC_WIDTH, tt, nsb, dyn_zero)
    yc = scp[:, 0:MIX_W] * exts_s[hs + rows:hs + 2 * rows, :]
    new_sc = exts_s[rows:rows + hs, :]
    exts_s[0:hs, :] = new_sc
    sco_ref[...] = new_sc.reshape(SC_WIDTH - 1, nsb, MIX_W)

    extp_s[hp:hp + rows, :] = v
    ext = extp_s[...]
    s2 = ext[nsb:, :] + ext[:-nsb, :]
    s4 = s2[2 * nsb:, :] + s2[:-2 * nsb, :]
    s8 = s4[4 * nsb:, :] + s4[:-4 * nsb, :]
    s16 = s8[8 * nsb:, :] + s8[:-8 * nsb, :]
    lane = lax.broadcasted_iota(jnp.int32, (1, MIX_W), 1)
    grp = lane // POOL_GW
    wsum = jnp.where(grp == 0, s2[14 * nsb:, :],
                     jnp.where(grp == 1, s4[12 * nsb:, :],
                               jnp.where(grp == 2, s8[8 * nsb:, :], s16)))
    win = jnp.where(grp == 0, 2.0, jnp.where(grp == 1, 4.0, jnp.where(grp == 2, 8.0, 16.0))).astype(f32)
    step = (lax.broadcasted_iota(jnp.int32, (rows, MIX_W), 0) >> (nsb.bit_length() - 1)) + t_idx * tt
    pos = step.astype(f32) + float(pos0)
    cnt = jnp.minimum(pos + 1.0, win)
    d = wsum / cnt - v
    ya = _dot(d.astype(bf16), wpool_ref[...]) * pscale_ref[...]
    new_pool = extp_s[rows:rows + hp, :]
    extp_s[0:hp, :] = new_pool
    po_ref[...] = new_pool.reshape(POOL_BUF, nsb, MIX_W)

    merged = (gate_a * _dot(ya.astype(bf16), wbr_ref[0:MIX_W, :])
              + gate_c * _dot(yc.astype(bf16), wbr_ref[2 * MIX_W:3 * MIX_W, :]))
    gate_d = gate(3)

    lam_r = jnp.broadcast_to(lam_ref[0:1, :], (SUBLANES, SSM_W))
    lam_i = jnp.broadcast_to(lam_ref[1:2, :], (SUBLANES, SSM_W))
    for j in range(nsb // SUBLANES):
        r0 = j * SUBLANES
        hr, hi = h_s[r0:r0 + SUBLANES, 0:SSM_W], h_s[r0:r0 + SUBLANES, SSM_W:]
        for t in range(tt):
            row = t * nsb + r0
            nr = lam_r * hr - lam_i * hi + bu_s[row:row + SUBLANES, 0:SSM_W]
            ni = lam_r * hi + lam_i * hr + bu_s[row:row + SUBLANES, SSM_W:]
            bu_s[row:row + SUBLANES, 0:SSM_W] = nr
            bu_s[row:row + SUBLANES, SSM_W:] = ni
            hr, hi = nr, ni
        h_s[r0:r0 + SUBLANES, 0:SSM_W] = hr
        h_s[r0:r0 + SUBLANES, SSM_W:] = hi
    reo_ref[...] = h_s[:, 0:SSM_W]
    imo_ref[...] = h_s[:, SSM_W:]
    y = (_dot(bu_s[:, 0:SSM_W].astype(bf16), cmat_ref[0:SSM_W, :])
         + _dot(bu_s[:, SSM_W:].astype(bf16), cmat_ref[SSM_W:, :]) + dskip_ref[...] * u)

    conv = extc_s[hc + rows:hc + 2 * rows, :]
    mu = jnp.mean(conv, axis=-1, keepdims=True)
    cen = conv - mu
    var = jnp.mean(cen * cen, axis=-1, keepdims=True)
    yn = cen * lax.rsqrt(var + LN_EPS) * lng_ref[...] + lnb_ref[...]
    yd = yn * _sigmoid(yn)
    merged = merged + gate_d * _dot(yd.astype(bf16), wbr_ref[3 * MIX_W:, :])
    gate_b = gate(1)

    z = _gelu_tanh(y)
    yb = z * _sigmoid(_dot(z.astype(bf16), wglu_ref[...]) + bglu_ref[...])
    merged = merged + gate_b * _dot(yb.astype(bf16), wbr_ref[MIX_W:2 * MIX_W, :])

    xo = x + _dot(merged.astype(bf16), wout_ref[...])
    xo_ref[...] = xo.reshape(tt, nsb, D_MODEL)


def _const_spec(block_shape, index):
    return pl.BlockSpec(block_shape, lambda i: index, pipeline_mode=pl.Buffered(1))


def _chunk_spec(rows_total, cols, layer):
    return pl.BlockSpec((None, rows_total // W_STEPS, cols), lambda i: (layer, jnp.minimum(i, W_STEPS - 1), 0))


def _chunk_out(rows_total, cols):
    return (pl.BlockSpec((rows_total // W_STEPS, cols), lambda i: (jnp.minimum(i, W_STEPS - 1), 0)),
            jax.ShapeDtypeStruct((rows_total, cols), bf16))


def _whole(shape):
    return pl.BlockSpec(shape, lambda i: (0,) * len(shape), pipeline_mode=pl.Buffered(1))


def _mixer_call(layer, x3, st, wts, *, tt, nsb, pos0, seq_major_in=False, bf16_weights=None):
    if seq_major_in:
        n_seq, t_total, _ = x3.shape
    else:
        t_total, n_seq, _ = x3.shape
    assert nsb % SUBLANES == 0 and nsb & (nsb - 1) == 0 and n_seq % nsb == 0 and t_total % tt == 0
    rows = tt * nsb
    n_s, n_t = n_seq // nsb, t_total // tt
    c = MIX_W

    w_steps = W_STEPS if bf16_weights is None else 0

    def tile(i):
        q = jnp.maximum(i - w_steps, 0)
        return q // n_t, q % n_t

    def hist_spec(h):
        return pl.BlockSpec((None, h, nsb, c), lambda i: (layer, 0, tile(i)[0], 0))

    ssm_spec = pl.BlockSpec((None, nsb, SSM_W), lambda i: (layer, tile(i)[0], 0))
    x_spec = pl.BlockSpec((tt, nsb, D_MODEL), lambda i: (tile(i)[1], tile(i)[0], 0))
    x_in_spec = pl.BlockSpec((nsb, tt, D_MODEL), lambda i: (tile(i)[0], tile(i)[1], 0)) if seq_major_in else x_spec
    l3 = (layer, 0, 0)
    in_specs = [
        pl.BlockSpec(memory_space=pltpu.SMEM),
        x_in_spec,
        *([hist_spec(POOL_BUF), ssm_spec, ssm_spec, hist_spec(SC_WIDTH - 1), hist_spec(CF_WIDTH - 1)] if st else []),
        _const_spec((None, 1, D_MODEL), l3),
        _chunk_spec(D_MODEL, IN_COLS, layer) if w_steps else _whole((D_MODEL, IN_COLS)),
        _const_spec((None, N_BRANCH, D_MODEL), l3),
        _const_spec((None, c, c), l3),
        _const_spec((None, 1, c), l3),
        _const_spec((None, 2, SSM_W), l3),
        _const_spec((None, c, 2 * SSM_W), l3),
        _const_spec((None, 2 * SSM_W, c), l3),
        _const_spec((None, 1, c), l3),
        _const_spec((None, c, c), l3),
        _const_spec((None, 1, c), l3),
        _const_spec((None, SC_WIDTH, SUBLANES, c), (layer, 0, 0, 0)),
        _const_spec((None, CF_WIDTH, SUBLANES, c), (layer, 0, 0, 0)),
        _const_spec((None, 1, c), l3),
        _const_spec((None, 1, c), l3),
        _chunk_spec(N_BRANCH * c, D_MODEL, layer) if w_steps else _whole((N_BRANCH * c, D_MODEL)),
        _chunk_spec(D_MODEL, D_MODEL, layer) if w_steps else _whole((D_MODEL, D_MODEL)),
    ]

    def hist_out(h):
        return pl.BlockSpec((h, nsb, c), lambda i: (0, tile(i)[0], 0))

    ssm_out = pl.BlockSpec((nsb, SSM_W), lambda i: (tile(i)[0], 0))
    out_specs = [x_spec, hist_out(POOL_BUF), ssm_out, ssm_out, hist_out(SC_WIDTH - 1), hist_out(CF_WIDTH - 1)]
    out_shape = [
        jax.ShapeDtypeStruct((t_total, n_seq, D_MODEL), f32),
        jax.ShapeDtypeStruct((POOL_BUF, n_seq, c), f32),
        jax.ShapeDtypeStruct((n_seq, SSM_W), f32),
        jax.ShapeDtypeStruct((n_seq, SSM_W), f32),
        jax.ShapeDtypeStruct((SC_WIDTH - 1, n_seq, c), f32),
        jax.ShapeDtypeStruct((CF_WIDTH - 1, n_seq, c), f32),
    ]
    w_shapes = ((D_MODEL, IN_COLS), (N_BRANCH * c, D_MODEL), (D_MODEL, D_MODEL))
    scratch = []
    if w_steps:
        for spec, shape in (_chunk_out(*ws) for ws in w_shapes):
            out_specs.append(spec)
            out_shape.append(shape)
        scratch = [pltpu.VMEM(ws, bf16) for ws in w_shapes]
    scratch += [
        pltpu.VMEM((rows, 2 * SSM_W), f32),
        pltpu.VMEM((POOL_BUF * nsb + rows, c), f32),
        pltpu.VMEM(((SC_WIDTH - 1) * nsb + 2 * rows, c), f32),
        pltpu.VMEM(((CF_WIDTH - 1) * nsb + 2 * rows, c), f32),
        pltpu.VMEM((nsb, 2 * SSM_W), f32),
    ]
    if seq_major_in:
        scratch.append(pltpu.VMEM((D_MODEL // LANES, rows, LANES), f32))
    kern = functools.partial(_mixer_kernel, tt=tt, nsb=nsb, n_t=n_t, pos0=pos0, seq_major_in=seq_major_in,
                             w_steps=w_steps, has_state=st is not None)
    if bf16_weights is not None:
        wts = list(wts)
        wts[1], wts[15], wts[16] = bf16_weights
    return pl.pallas_call(
        kern, grid=(w_steps + n_s * n_t,), in_specs=in_specs, out_specs=out_specs, out_shape=out_shape,
        scratch_shapes=scratch,
        compiler_params=pltpu.CompilerParams(dimension_semantics=("arbitrary",),
                                             vmem_limit_bytes=VMEM_LIMIT_BYTES),
        name=f"mixer_l{layer}_n{nsb}",
    )(jnp.zeros((1,), jnp.int32), x3, *(st or ()), *wts)


def _ffn_kernel(x_ref, n2_ref, wfi_in, wfo_in, fg_ref, o_ref, *rest, final, seq_major_out, w_steps):
    step = pl.program_id(0)
    if w_steps:
        (wfi_o, wfo_o, wfi_ref, wfo_ref, *maybe_t_s) = rest

        @pl.when(step < w_steps)
        def _cast_weights():
            for chunk_ref, dst, out in ((wfi_in, wfi_ref, wfi_o), (wfo_in, wfo_ref, wfo_o)):
                n = chunk_ref.shape[0]
                chunk = chunk_ref[...].astype(bf16)
                dst[pl.ds(pl.multiple_of(step * n, n), n), :] = chunk
                out[...] = chunk
    else:
        wfi_ref, wfo_ref, maybe_t_s = wfi_in, wfo_in, rest

    @pl.when(step >= w_steps)
    def _tile():
        _ffn_tile(x_ref, n2_ref, wfi_ref, wfo_ref, fg_ref, o_ref, *maybe_t_s,
                  final=final, seq_major_out=seq_major_out)


def _ffn_tile(x_ref, n2_ref, wfi_ref, wfo_ref, fg_ref, o_ref, *maybe_t_s, final, seq_major_out):
    x = x_ref[...]
    hn = _rmsnorm(x, n2_ref[...]).astype(bf16)
    acc = x
    for lo, hi in FFN_COL_CHUNKS:
        g = _dot(hn, wfi_ref[:, lo:hi])
        up = _dot(hn, wfi_ref[:, FF_DIM + lo:FF_DIM + hi])
        act = (g * _sigmoid(g)) * up
        acc = acc + _dot(act.astype(bf16), wfo_ref[lo:hi, :])
    if final:
        acc = _rmsnorm(acc, fg_ref[...])
    if seq_major_out:
        (t_s,) = maybe_t_s
        n_seq, steps, _ = o_ref.shape
        for j in range(D_MODEL // LANES):
            t_s[j] = acc[:, j * LANES:(j + 1) * LANES]
        for n in range(n_seq):
            for j in range(D_MODEL // LANES):
                o_ref[n, :, j * LANES:(j + 1) * LANES] = t_s[j, pl.ds(n, steps, stride=n_seq), :]
    else:
        o_ref[...] = acc


def _ffn_call(layer, x2, n2, wfi, wfo, fg, *, rows, final, seq_major_out=0, bf16_weights=None):
    total = x2.shape[0]
    w_steps = W_STEPS if bf16_weights is None else 0
    tile = lambda i: jnp.maximum(i - w_steps, 0)
    x_spec = pl.BlockSpec((rows, D_MODEL), lambda i: (tile(i), 0))
    w_shapes = ((D_MODEL, 2 * FF_DIM), (FF_DIM, D_MODEL))
    if seq_major_out:
        steps = rows // seq_major_out
        out_specs = [pl.BlockSpec((seq_major_out, steps, D_MODEL), lambda i: (0, tile(i), 0))]
        out_shape = [jax.ShapeDtypeStruct((seq_major_out, total // seq_major_out, D_MODEL), f32)]
    else:
        out_specs, out_shape = [x_spec], [jax.ShapeDtypeStruct(x2.shape, f32)]
    scratch = []
    if w_steps:
        for spec, shape in (_chunk_out(*ws) for ws in w_shapes):
            out_specs.append(spec)
            out_shape.append(shape)
        scratch = [pltpu.VMEM(ws, bf16) for ws in w_shapes]
        w_specs = [_chunk_spec(D_MODEL, 2 * FF_DIM, layer), _chunk_spec(FF_DIM, D_MODEL, layer)]
    else:
        wfi, wfo = bf16_weights
        w_specs = [_whole(ws) for ws in w_shapes]
    if seq_major_out:
        scratch.append(pltpu.VMEM((D_MODEL // LANES, rows, LANES), f32))

    outs = pl.pallas_call(
        functools.partial(_ffn_kernel, final=final, seq_major_out=seq_major_out, w_steps=w_steps),
        grid=(w_steps + total // rows,),
        in_specs=[x_spec, _const_spec((None, 1, D_MODEL), (layer, 0, 0)), *w_specs, _const_spec((1, D_MODEL), (0, 0))],
        out_specs=out_specs,
        out_shape=out_shape,
        scratch_shapes=scratch,
        compiler_params=pltpu.CompilerParams(dimension_semantics=("arbitrary",),
                                             vmem_limit_bytes=VMEM_LIMIT_BYTES),
        name=f"ffn_l{layer}_r{total}",
    )(x2, n2, wfi, wfo, fg)
    return outs[0], (tuple(outs[1:]) if w_steps else None)


def _ssm_params(lam_re, lam_im, log_dt, b_re, b_im, c_re, c_im):
    g, p, gc = SSM_GROUPS, SSM_STATE, SSM_GROUP_CH
    dt = jnp.exp(log_dt)[..., None]
    mag = jnp.exp(lam_re * dt)
    lbr, lbi = mag * jnp.cos(lam_im * dt), mag * jnp.sin(lam_im * dt)
    den = lam_re * lam_re + lam_im * lam_im
    qr = ((lbr - 1.0) * lam_re + lbi * lam_im) / den
    qi = (lbi * lam_re - (lbr - 1.0) * lam_im) / den
    bbr = qr[..., None] * b_re - qi[..., None] * b_im
    bbi = qr[..., None] * b_im + qi[..., None] * b_re
    eye = jnp.eye(g, dtype=f32)

    def to_bmat(b):
        bt = jnp.transpose(b, (0, 1, 3, 2))
        return (bt[:, :, :, None, :] * eye[None, :, None, :, None]).reshape(DEPTH, g * gc, g * p)

    def to_cmat(cm):
        ct = jnp.transpose(cm, (0, 1, 3, 2))
        return (ct[:, :, :, None, :] * eye[None, :, None, :, None]).reshape(DEPTH, g * p, g * gc)

    bmat = jnp.concatenate([to_bmat(bbr), to_bmat(bbi)], axis=-1).astype(bf16)
    cmat = jnp.concatenate([to_cmat(c_re), to_cmat(-c_im)], axis=-2).astype(bf16)
    lam = jnp.stack([lbr.reshape(DEPTH, g * p), lbi.reshape(DEPTH, g * p)], axis=1)
    return lam, bmat, cmat


def _pool_blockdiag(pool_w):
    k = len(POOL_WINDOWS)
    eye = jnp.eye(k, dtype=f32)
    w = pool_w[:, :, :, None, :] * eye[None, :, None, :, None]
    return w.reshape(DEPTH, MIX_W, MIX_W).astype(bf16)


def _time_major_state(s):
    return jnp.transpose(s, (0, 2, 1, 3))


def kernel(x_prompt, x_sample, state_pool, state_ssm_re, state_ssm_im, state_shortconv, state_conformer,
           norm1_g, norm2_g, final_g, w_in, b_gate, pool_w, pool_scale, lam_re, lam_im, log_dt,
           b_re, b_im, c_re, c_im, d_skip, w_glu, b_glu, sc_w, cf_w, cf_ln_g, cf_ln_b,
           w_branch, w_out, w_ffn_in, w_ffn_out):
    nb = x_prompt.shape[0]
    lam, bmat, cmat = _ssm_params(lam_re, lam_im, log_dt, b_re, b_im, c_re, c_im)
    row = lambda a: a.reshape(DEPTH, 1, a.shape[-1])
    taps = lambda w: jnp.broadcast_to(w[:, :, None, :], w.shape[:2] + (SUBLANES, MIX_W))
    mixer_w = (row(norm1_g), w_in, b_gate, _pool_blockdiag(pool_w), row(pool_scale),
               lam, bmat, cmat, row(d_skip), w_glu.astype(bf16), row(b_glu), taps(sc_w), taps(cf_w),
               row(cf_ln_g), row(cf_ln_b), w_branch.reshape(DEPTH, N_BRANCH * MIX_W, D_MODEL), w_out)
    n2 = row(norm2_g)
    wfi, wfo = w_ffn_in, w_ffn_out
    fg = final_g.reshape(1, D_MODEL)

    groups = []
    groups.append(dict(x=x_prompt, seq_major=True, pos0=0, tt=64, nsb=nb, st=None))
    ns = x_sample.shape[0]
    groups.append(dict(x=jnp.transpose(x_sample, (1, 0, 2)), seq_major=False, pos0=PAST_LEN,
                       tt=x_sample.shape[1], nsb=64,
                       st=(_time_major_state(state_pool), state_ssm_re.reshape(DEPTH, ns, SSM_W),
                           state_ssm_im.reshape(DEPTH, ns, SSM_W), _time_major_state(state_shortconv),
                           _time_major_state(state_conformer))))

    results = []
    cast_mixer, cast_ffn = [None] * DEPTH, [None] * DEPTH
    for grp in groups:
        x3, seq_major = grp["x"], grp["seq_major"]
        outs = [[] for _ in range(5)]
        for layer in range(DEPTH):
            last = layer == DEPTH - 1
            res = _mixer_call(layer, x3, grp["st"], mixer_w, tt=grp["tt"], nsb=grp["nsb"], pos0=grp["pos0"],
                              seq_major_in=seq_major and layer == 0, bf16_weights=cast_mixer[layer])
            x3, po, reo, imo, sco, cfo = res[:6]
            if cast_mixer[layer] is None:
                cast_mixer[layer] = tuple(res[6:])
            t_total, n_seq, _ = x3.shape
            for lst, o in zip(outs, (po, reo, imo, sco, cfo)):
                lst.append(o)
            x2, cast = _ffn_call(layer, x3.reshape(t_total * n_seq, D_MODEL), n2, wfi, wfo, fg, rows=FFN_ROWS,
                                 final=last, seq_major_out=n_seq if (seq_major and last) else 0,
                                 bf16_weights=cast_ffn[layer])
            if cast_ffn[layer] is None:
                cast_ffn[layer] = cast
            x3 = x2 if (seq_major and last) else x2.reshape(t_total, n_seq, D_MODEL)
        y = x3 if seq_major else jnp.transpose(x3, (1, 0, 2))
        pool = jnp.transpose(jnp.stack(outs[0]), (0, 2, 1, 3))
        sre = jnp.stack(outs[1]).reshape(DEPTH, n_seq, SSM_GROUPS, SSM_STATE)
        sim = jnp.stack(outs[2]).reshape(DEPTH, n_seq, SSM_GROUPS, SSM_STATE)
        sc = jnp.transpose(jnp.stack(outs[3]), (0, 2, 1, 3))
        cf = jnp.transpose(jnp.stack(outs[4]), (0, 2, 1, 3))
        results.append((y, pool, sre, sim, sc, cf))

    (yp, pp, rp, ip, cp, fp), (ys, ps, rs, is_, cs, fs) = results
    return (yp, ys, pp, ps, rp, rs, ip, is_, cp, cs, fp, fs)
```

```python
import functools
import math

import jax
import jax.numpy as jnp
from jax import lax
from jax.experimental import pallas as pl
from jax.experimental.pallas import tpu as pltpu

D_MODEL = 1024
DEPTH = 4
PAST_LEN = 16384
N_BRANCH = 4
MIX_W = D_MODEL // 4
POOL_WINDOWS = (2, 4, 8, 16)
POOL_GW = MIX_W // len(POOL_WINDOWS)
POOL_BUF = max(POOL_WINDOWS) - 1
SSM_GROUP_CH = 16
SSM_GROUPS = MIX_W // SSM_GROUP_CH
SSM_STATE = 64
SSM_W = SSM_GROUPS * SSM_STATE
SC_WIDTH = 3
CF_WIDTH = 31
FF_DIM = -(-8 * D_MODEL // (3 * 256)) * 256
OFF_POOL = 0
OFF_SSM = OFF_POOL + MIX_W
OFF_SC = OFF_SSM + MIX_W
OFF_CF = OFF_SC + 3 * MIX_W
OFF_GATE = OFF_CF + 2 * MIX_W
IN_COLS = OFF_GATE + N_BRANCH * D_MODEL
RMS_EPS = 1e-6
LN_EPS = 1e-5

SUBLANES = 8
VMEM_LIMIT_BYTES = 56 * 1024 * 1024
FFN_COL_CHUNKS = ((0, 1024), (1024, 2048), (2048, FF_DIM))
FFN_ROWS = 512
W_STEPS = 8
FFN_ROW_SPLIT = 2
LANES = 128
CONV_TIME_BLOCK = 2

f32 = jnp.float32
bf16 = jnp.bfloat16


def _sigmoid(x):
    return 0.5 * jnp.tanh(0.5 * x) + 0.5


def _gelu_tanh(x):
    c = math.sqrt(2.0 / math.pi)
    return x * (0.5 * (1.0 + jnp.tanh(c * (x + 0.044715 * (x * x * x)))))


def _rmsnorm(x, g):
    y = x * lax.rsqrt(jnp.mean(x * x, axis=-1, keepdims=True) + RMS_EPS)
    return y * g


def _dot(a, b):
    return jnp.dot(a, b, preferred_element_type=f32)


def _causal_conv(ext_s, w_ref, width, tt, nsb, dyn_zero):
    tb = min(tt, CONV_TIME_BLOCK)
    assert tt % tb == 0
    out0 = (width - 1 + tt) * nsb
    for l0 in range(0, MIX_W, LANES):
        for r0 in range(0, nsb, SUBLANES):
            for t0 in range(0, tt, tb):
                acc = [None] * tb
                for e in range(tb + width - 1):
                    row = pl.multiple_of((t0 + e) * nsb + r0 + dyn_zero, SUBLANES)
                    xe = ext_s[pl.ds(row, SUBLANES), l0:l0 + LANES]
                    for b in range(max(0, e - width + 1), min(tb, e + 1)):
                        term = w_ref[e - b, :, l0:l0 + LANES] * xe
                        acc[b] = term if acc[b] is None else acc[b] + term
                for b in range(tb):
                    row = pl.multiple_of(out0 + (t0 + b) * nsb + r0 + dyn_zero, SUBLANES)
                    ext_s[pl.ds(row, SUBLANES), l0:l0 + LANES] = acc[b]


def _mixer_kernel(zero_ref, x_ref, *refs, tt, nsb, n_t, pos0, seq_major_in, w_steps, has_state):
    state_refs, refs = (refs[:5], refs[5:]) if has_state else ((None,) * 5, refs)
    sp_ref, sre_ref, sim_ref, ssc_ref, scf_ref = state_refs
    (n1_ref, win_in, bgate_ref, wpool_ref, pscale_ref, lam_ref, bmat_ref, cmat_ref,
     dskip_ref, wglu_ref, bglu_ref, scw_ref, cfw_ref, lng_ref, lnb_ref, wbr_in, wout_in,
     xo_ref, po_ref, reo_ref, imo_ref, sco_ref, cfo_ref, *rest) = refs
    step = pl.program_id(0)
    if w_steps:
        (win_o, wbr_o, wout_o, win_ref, wbr_ref, wout_ref, *scratch) = rest

        @pl.when(step < w_steps)
        def _cast_weights():
            for chunk_ref, dst, out in ((win_in, win_ref, win_o), (wbr_in, wbr_ref, wbr_o), (wout_in, wout_ref, wout_o)):
                n = chunk_ref.shape[0]
                chunk = chunk_ref[...].astype(bf16)
                dst[pl.ds(pl.multiple_of(step * n, n), n), :] = chunk
                out[...] = chunk
    else:
        win_ref, wbr_ref, wout_ref, scratch = win_in, wbr_in, wout_in, rest

    @pl.when(step >= w_steps)
    def _tile():
        _mixer_tile(zero_ref, x_ref, sp_ref, sre_ref, sim_ref, ssc_ref, scf_ref,
                    n1_ref, win_ref, bgate_ref, wpool_ref, pscale_ref, lam_ref, bmat_ref, cmat_ref,
                    dskip_ref, wglu_ref, bglu_ref, scw_ref, cfw_ref, lng_ref, lnb_ref, wbr_ref, wout_ref,
                    xo_ref, po_ref, reo_ref, imo_ref, sco_ref, cfo_ref, *scratch,
                    t_idx=(step - w_steps) % n_t, tt=tt, nsb=nsb, pos0=pos0, seq_major_in=seq_major_in)


def _mixer_tile(zero_ref, x_ref, sp_ref, sre_ref, sim_ref, ssc_ref, scf_ref,
                n1_ref, win_ref, bgate_ref, wpool_ref, pscale_ref, lam_ref, bmat_ref, cmat_ref,
                dskip_ref, wglu_ref, bglu_ref, scw_ref, cfw_ref, lng_ref, lnb_ref, wbr_ref, wout_ref,
                xo_ref, po_ref, reo_ref, imo_ref, sco_ref, cfo_ref,
                bu_s, extp_s, exts_s, extc_s, h_s, *maybe_xt_s,
                t_idx, tt, nsb, pos0, seq_major_in):
    rows = tt * nsb
    hp, hs, hc = POOL_BUF * nsb, (SC_WIDTH - 1) * nsb, (CF_WIDTH - 1) * nsb
    dyn_zero = zero_ref[0]

    @pl.when(t_idx == 0)
    def _load_state():
        if sp_ref is None:
            extp_s[0:hp, :] = jnp.zeros((hp, MIX_W), f32)
            exts_s[0:hs, :] = jnp.zeros((hs, MIX_W), f32)
            extc_s[0:hc, :] = jnp.zeros((hc, MIX_W), f32)
            h_s[...] = jnp.zeros(h_s.shape, f32)
        else:
            extp_s[0:hp, :] = sp_ref[...].reshape(hp, MIX_W)
            exts_s[0:hs, :] = ssc_ref[...].reshape(hs, MIX_W)
            extc_s[0:hc, :] = scf_ref[...].reshape(hc, MIX_W)
            h_s[:, 0:SSM_W] = sre_ref[...]
            h_s[:, SSM_W:] = sim_ref[...]

    if seq_major_in:
        (xt_s,) = maybe_xt_s
        for n in range(nsb):
            for j in range(D_MODEL // LANES):
                xt_s[j, pl.ds(n, tt, stride=nsb), :] = x_ref[n, :, j * LANES:(j + 1) * LANES]
        x = jnp.concatenate([xt_s[j] for j in range(D_MODEL // LANES)], axis=1)
    else:
        x = x_ref[...].reshape(rows, D_MODEL)
    hn = _rmsnorm(x, n1_ref[...]).astype(bf16)

    def proj(lo, hi):
        return _dot(hn, win_ref[:, lo:hi])

    def gate(k):
        lo = OFF_GATE + k * D_MODEL
        return _sigmoid(proj(lo, lo + D_MODEL) + bgate_ref[k:k + 1, :])

    vu = proj(OFF_POOL, OFF_SC)
    v, u = vu[:, 0:MIX_W], vu[:, MIX_W:]
    ab = proj(OFF_CF, OFF_GATE)
    bu_s[...] = _dot(u.astype(bf16), bmat_ref[...])
    scp = proj(OFF_SC, OFF_CF)
    gate_c = gate(2)
    gate_a = gate(0)

    extc_s[hc:hc + rows, :] = ab[:, 0:MIX_W] * _sigmoid(ab[:, MIX_W:])
    _causal_conv(extc_s, cfw_ref, CF_WIDTH, tt, nsb, dyn_zero)
    new_cf = extc_s[rows:rows + hc, :]
    extc_s[0:hc, :] = new_cf
    cfo_ref[...] = new_cf.reshape(CF_WIDTH - 1, nsb, MIX_W)

    exts_s[hs:hs + rows, :] = scp[:, MIX_W:2 * MIX_W] * scp[:, 2 * MIX_W:]
    _causal_conv(exts_s, scw_ref, SC_WIDTH, tt, nsb, dyn_zero)
    yc = scp[:, 0:MIX_W] * exts_s[hs + rows:hs + 2 * rows, :]
    new_sc = exts_s[rows:rows + hs, :]
    exts_s[0:hs, :] = new_sc
    sco_ref[...] = new_sc.reshape(SC_WIDTH - 1, nsb, MIX_W)

    extp_s[hp:hp + rows, :] = v
    ext = extp_s[...]
    s2 = ext[nsb:, :] + ext[:-nsb, :]
    s4 = s2[2 * nsb:, :] + s2[:-2 * nsb, :]
    s8 = s4[4 * nsb:, :] + s4[:-4 * nsb, :]
    s16 = s8[8 * nsb:, :] + s8[:-8 * nsb, :]
    lane = lax.broadcasted_iota(jnp.int32, (1, MIX_W), 1)
    grp = lane // POOL_GW
    wsum = jnp.where(grp == 0, s2[14 * nsb:, :],
                     jnp.where(grp == 1, s4[12 * nsb:, :],
                               jnp.where(grp == 2, s8[8 * nsb:, :], s16)))
    win = jnp.where(grp == 0, 2.0, jnp.where(grp == 1, 4.0, jnp.where(grp == 2, 8.0, 16.0))).astype(f32)
    step = (lax.broadcasted_iota(jnp.int32, (rows, MIX_W), 0) >> (nsb.bit_length() - 1)) + t_idx * tt
    pos = step.astype(f32) + float(pos0)
    cnt = jnp.minimum(pos + 1.0, win)
    d = wsum / cnt - v
    ya = _dot(d.astype(bf16), wpool_ref[...]) * pscale_ref[...]
    new_pool = extp_s[rows:rows + hp, :]
    extp_s[0:hp, :] = new_pool
    po_ref[...] = new_pool.reshape(POOL_BUF, nsb, MIX_W)

    merged = (gate_a * _dot(ya.astype(bf16), wbr_ref[0:MIX_W, :])
              + gate_c * _dot(yc.astype(bf16), wbr_ref[2 * MIX_W:3 * MIX_W, :]))
    gate_d = gate(3)

    lam_r = jnp.broadcast_to(lam_ref[0:1, :], (SUBLANES, SSM_W))
    lam_i = jnp.broadcast_to(lam_ref[1:2, :], (SUBLANES, SSM_W))
    for j in range(nsb // SUBLANES):
        r0 = j * SUBLANES
        hr, hi = h_s[r0:r0 + SUBLANES, 0:SSM_W], h_s[r0:r0 + SUBLANES, SSM_W:]
        for t in range(tt):
            row = t * nsb + r0
            nr = lam_r * hr - lam_i * hi + bu_s[row:row + SUBLANES, 0:SSM_W]
            ni = lam_r * hi + lam_i * hr + bu_s[row:row + SUBLANES, SSM_W:]
            bu_s[row:row + SUBLANES, 0:SSM_W] = nr
            bu_s[row:row + SUBLANES, SSM_W:] = ni
            hr, hi = nr, ni
        h_s[r0:r0 + SUBLANES, 0:SSM_W] = hr
        h_s[r0:r0 + SUBLANES, SSM_W:] = hi
    reo_ref[...] = h_s[:, 0:SSM_W]
    imo_ref[...] = h_s[:, SSM_W:]
    y = (_dot(bu_s[:, 0:SSM_W].astype(bf16), cmat_ref[0:SSM_W, :])
         + _dot(bu_s[:, SSM_W:].astype(bf16), cmat_ref[SSM_W:, :]) + dskip_ref[...] * u)

    conv = extc_s[hc + rows:hc + 2 * rows, :]
    mu = jnp.mean(conv, axis=-1, keepdims=True)
    cen = conv - mu
    var = jnp.mean(cen * cen, axis=-1, keepdims=True)
    yn = cen * lax.rsqrt(var + LN_EPS) * lng_ref[...] + lnb_ref[...]
    yd = yn * _sigmoid(yn)
    merged = merged + gate_d * _dot(yd.astype(bf16), wbr_ref[3 * MIX_W:, :])
    gate_b = gate(1)

    z = _gelu_tanh(y)
    yb = z * _sigmoid(_dot(z.astype(bf16), wglu_ref[...]) + bglu_ref[...])
    merged = merged + gate_b * _dot(yb.astype(bf16), wbr_ref[MIX_W:2 * MIX_W, :])

    xo = x + _dot(merged.astype(bf16), wout_ref[...])
    xo_ref[...] = xo.reshape(tt, nsb, D_MODEL)


def _const_spec(block_shape, index):
    return pl.BlockSpec(block_shape, lambda i: index, pipeline_mode=pl.Buffered(1))


def _chunk_spec(rows_total, cols, layer):
    return pl.BlockSpec((None, rows_total // W_STEPS, cols), lambda i: (layer, jnp.minimum(i, W_STEPS - 1), 0))


def _chunk_out(rows_total, cols):
    return (pl.BlockSpec((rows_total // W_STEPS, cols), lambda i: (jnp.minimum(i, W_STEPS - 1), 0)),
            jax.ShapeDtypeStruct((rows_total, cols), bf16))


def _whole(shape):
    return pl.BlockSpec(shape, lambda i: (0,) * len(shape), pipeline_mode=pl.Buffered(1))


def _mixer_call(layer, x3, st, wts, *, tt, nsb, pos0, seq_major_in=False, bf16_weights=None):
    if seq_major_in:
        n_seq, t_total, _ = x3.shape
    else:
        t_total, n_seq, _ = x3.shape
    assert nsb % SUBLANES == 0 and nsb & (nsb - 1) == 0 and n_seq % nsb == 0 and t_total % tt == 0
    rows = tt * nsb
    n_s, n_t = n_seq // nsb, t_total // tt
    c = MIX_W

    w_steps = W_STEPS if bf16_weights is None else 0

    def tile(i):
        q = jnp.maximum(i - w_steps, 0)
        return q // n_t, q % n_t

    def hist_spec(h):
        return pl.BlockSpec((None, h, nsb, c), lambda i: (layer, 0, tile(i)[0], 0))

    ssm_spec = pl.BlockSpec((None, nsb, SSM_W), lambda i: (layer, tile(i)[0], 0))
    x_spec = pl.BlockSpec((tt, nsb, D_MODEL), lambda i: (tile(i)[1], tile(i)[0], 0))
    x_in_spec = pl.BlockSpec((nsb, tt, D_MODEL), lambda i: (tile(i)[0], tile(i)[1], 0)) if seq_major_in else x_spec
    l3 = (layer, 0, 0)
    in_specs = [
        pl.BlockSpec(memory_space=pltpu.SMEM),
        x_in_spec,
        *([hist_spec(POOL_BUF), ssm_spec, ssm_spec, hist_spec(SC_WIDTH - 1), hist_spec(CF_WIDTH - 1)] if st else []),
        _const_spec((None, 1, D_MODEL), l3),
        _chunk_spec(D_MODEL, IN_COLS, layer) if w_steps else _whole((D_MODEL, IN_COLS)),
        _const_spec((None, N_BRANCH, D_MODEL), l3),
        _const_spec((None, c, c), l3),
        _const_spec((None, 1, c), l3),
        _const_spec((None, 2, SSM_W), l3),
        _const_spec((None, c, 2 * SSM_W), l3),
        _const_spec((None, 2 * SSM_W, c), l3),
        _const_spec((None, 1, c), l3),
        _const_spec((None, c, c), l3),
        _const_spec((None, 1, c), l3),
        _const_spec((None, SC_WIDTH, SUBLANES, c), (layer, 0, 0, 0)),
        _const_spec((None, CF_WIDTH, SUBLANES, c), (layer, 0, 0, 0)),
        _const_spec((None, 1, c), l3),
        _const_spec((None, 1, c), l3),
        _chunk_spec(N_BRANCH * c, D_MODEL, layer) if w_steps else _whole((N_BRANCH * c, D_MODEL)),
        _chunk_spec(D_MODEL, D_MODEL, layer) if w_steps else _whole((D_MODEL, D_MODEL)),
    ]

    def hist_out(h):
        return pl.BlockSpec((h, nsb, c), lambda i: (0, tile(i)[0], 0))

    ssm_out = pl.BlockSpec((nsb, SSM_W), lambda i: (tile(i)[0], 0))
    out_specs = [x_spec, hist_out(POOL_BUF), ssm_out, ssm_out, hist_out(SC_WIDTH - 1), hist_out(CF_WIDTH - 1)]
    out_shape = [
        jax.ShapeDtypeStruct((t_total, n_seq, D_MODEL), f32),
        jax.ShapeDtypeStruct((POOL_BUF, n_seq, c), f32),
        jax.ShapeDtypeStruct((n_seq, SSM_W), f32),
        jax.ShapeDtypeStruct((n_seq, SSM_W), f32),
        jax.ShapeDtypeStruct((SC_WIDTH - 1, n_seq, c), f32),
        jax.ShapeDtypeStruct((CF_WIDTH - 1, n_seq, c), f32),
    ]
    w_shapes = ((D_MODEL, IN_COLS), (N_BRANCH * c, D_MODEL), (D_MODEL, D_MODEL))
    scratch = []
    if w_steps:
        for spec, shape in (_chunk_out(*ws) for ws in w_shapes):
            out_specs.append(spec)
            out_shape.append(shape)
        scratch = [pltpu.VMEM(ws, bf16) for ws in w_shapes]
    scratch += [
        pltpu.VMEM((rows, 2 * SSM_W), f32),
        pltpu.VMEM((POOL_BUF * nsb + rows, c), f32),
        pltpu.VMEM(((SC_WIDTH - 1) * nsb + 2 * rows, c), f32),
        pltpu.VMEM(((CF_WIDTH - 1) * nsb + 2 * rows, c), f32),
        pltpu.VMEM((nsb, 2 * SSM_W), f32),
    ]
    if seq_major_in:
        scratch.append(pltpu.VMEM((D_MODEL // LANES, rows, LANES), f32))
    kern = functools.partial(_mixer_kernel, tt=tt, nsb=nsb, n_t=n_t, pos0=pos0, seq_major_in=seq_major_in,
                             w_steps=w_steps, has_state=st is not None)
    if bf16_weights is not None:
        wts = list(wts)
        wts[1], wts[15], wts[16] = bf16_weights
    return pl.pallas_call(
        kern, grid=(w_steps + n_s * n_t,), in_specs=in_specs, out_specs=out_specs, out_shape=out_shape,
        scratch_shapes=scratch,
        compiler_params=pltpu.CompilerParams(dimension_semantics=("arbitrary",),
                                             vmem_limit_bytes=VMEM_LIMIT_BYTES),
        name=f"mixer_l{layer}_n{nsb}",
    )(jnp.zeros((1,), jnp.int32), x3, *(st or ()), *wts)


def _ffn_kernel(x_ref, n2_ref, wfi_in, wfo_in, fg_ref, o_ref, *rest, final, seq_major_out, w_steps):
    step = pl.program_id(0)
    if w_steps:
        (wfi_o, wfo_o, wfi_ref, wfo_ref, *maybe_t_s) = rest

        @pl.when(step < w_steps)
        def _cast_weights():
            for chunk_ref, dst, out in ((wfi_in, wfi_ref, wfi_o), (wfo_in, wfo_ref, wfo_o)):
                n = chunk_ref.shape[0]
                chunk = chunk_ref[...].astype(bf16)
                dst[pl.ds(pl.multiple_of(step * n, n), n), :] = chunk
                out[...] = chunk
    else:
        wfi_ref, wfo_ref, maybe_t_s = wfi_in, wfo_in, rest

    @pl.when(step >= w_steps)
    def _tile():
        _ffn_tile(x_ref, n2_ref, wfi_ref, wfo_ref, fg_ref, o_ref, *maybe_t_s,
                  final=final, seq_major_out=seq_major_out)


def _ffn_tile(x_ref, n2_ref, wfi_ref, wfo_ref, fg_ref, o_ref, *maybe_t_s, final, seq_major_out):
    rows = x_ref.shape[0]
    part = rows // FFN_ROW_SPLIT
    for p in range(FFN_ROW_SPLIT):
        r0 = p * part
        x = x_ref[r0:r0 + part, :]
        hn = _rmsnorm(x, n2_ref[...]).astype(bf16)
        acc = x
        for lo, hi in FFN_COL_CHUNKS:
            g = _dot(hn, wfi_ref[:, lo:hi])
            up = _dot(hn, wfi_ref[:, FF_DIM + lo:FF_DIM + hi])
            act = (g * _sigmoid(g)) * up
            acc = acc + _dot(act.astype(bf16), wfo_ref[lo:hi, :])
        if final:
            acc = _rmsnorm(acc, fg_ref[...])
        if seq_major_out:
            (t_s,) = maybe_t_s
            n_seq, steps, _ = o_ref.shape
            psteps = steps // FFN_ROW_SPLIT
            for j in range(D_MODEL // LANES):
                t_s[j, r0:r0 + part, :] = acc[:, j * LANES:(j + 1) * LANES]
            for n in range(n_seq):
                for j in range(D_MODEL // LANES):
                    o_ref[n, p * psteps:(p + 1) * psteps, j * LANES:(j + 1) * LANES] = (
                        t_s[j, pl.ds(r0 + n, psteps, stride=n_seq), :])
        else:
            o_ref[r0:r0 + part, :] = acc


def _ffn_call(layer, x2, n2, wfi, wfo, fg, *, rows, final, seq_major_out=0, bf16_weights=None):
    total = x2.shape[0]
    w_steps = W_STEPS if bf16_weights is None else 0
    tile = lambda i: jnp.maximum(i - w_steps, 0)
    x_spec = pl.BlockSpec((rows, D_MODEL), lambda i: (tile(i), 0))
    w_shapes = ((D_MODEL, 2 * FF_DIM), (FF_DIM, D_MODEL))
    if seq_major_out:
        steps = rows // seq_major_out
        out_specs = [pl.BlockSpec((seq_major_out, steps, D_MODEL), lambda i: (0, tile(i), 0))]
        out_shape = [jax.ShapeDtypeStruct((seq_major_out, total // seq_major_out, D_MODEL), f32)]
    else:
        out_specs, out_shape = [x_spec], [jax.ShapeDtypeStruct(x2.shape, f32)]
    scratch = []
    if w_steps:
        for spec, shape in (_chunk_out(*ws) for ws in w_shapes):
            out_specs.append(spec)
            out_shape.append(shape)
        scratch = [pltpu.VMEM(ws, bf16) for ws in w_shapes]
        w_specs = [_chunk_spec(D_MODEL, 2 * FF_DIM, layer), _chunk_spec(FF_DIM, D_MODEL, layer)]
    else:
        wfi, wfo = bf16_weights
        w_specs = [_whole(ws) for ws in w_shapes]
    if seq_major_out:
        scratch.append(pltpu.VMEM((D_MODEL // LANES, rows, LANES), f32))

    outs = pl.pallas_call(
        functools.partial(_ffn_kernel, final=final, seq_major_out=seq_major_out, w_steps=w_steps),
        grid=(w_steps + total // rows,),
        in_specs=[x_spec, _const_spec((None, 1, D_MODEL), (layer, 0, 0)), *w_specs, _const_spec((1, D_MODEL), (0, 0))],
        out_specs=out_specs,
        out_shape=out_shape,
        scratch_shapes=scratch,
        compiler_params=pltpu.CompilerParams(dimension_semantics=("arbitrary",),
                                             vmem_limit_bytes=VMEM_LIMIT_BYTES),
        name=f"ffn_l{layer}_r{total}",
    )(x2, n2, wfi, wfo, fg)
    return outs[0], (tuple(outs[1:]) if w_steps else None)


def _ssm_params(lam_re, lam_im, log_dt, b_re, b_im, c_re, c_im):
    g, p, gc = SSM_GROUPS, SSM_STATE, SSM_GROUP_CH
    dt = jnp.exp(log_dt)[..., None]
    mag = jnp.exp(lam_re * dt)
    lbr, lbi = mag * jnp.cos(lam_im * dt), mag * jnp.sin(lam_im * dt)
    den = lam_re * lam_re + lam_im * lam_im
    qr = ((lbr - 1.0) * lam_re + lbi * lam_im) / den
    qi = (lbi * lam_re - (lbr - 1.0) * lam_im) / den
    bbr = qr[..., None] * b_re - qi[..., None] * b_im
    bbi = qr[..., None] * b_im + qi[..., None] * b_re
    eye = jnp.eye(g, dtype=f32)

    def to_bmat(b):
        bt = jnp.transpose(b, (0, 1, 3, 2))
        return (bt[:, :, :, None, :] * eye[None, :, None, :, None]).reshape(DEPTH, g * gc, g * p)

    def to_cmat(cm):
        ct = jnp.transpose(cm, (0, 1, 3, 2))
        return (ct[:, :, :, None, :] * eye[None, :, None, :, None]).reshape(DEPTH, g * p, g * gc)

    bmat = jnp.concatenate([to_bmat(bbr), to_bmat(bbi)], axis=-1).astype(bf16)
    cmat = jnp.concatenate([to_cmat(c_re), to_cmat(-c_im)], axis=-2).astype(bf16)
    lam = jnp.stack([lbr.reshape(DEPTH, g * p), lbi.reshape(DEPTH, g * p)], axis=1)
    return lam, bmat, cmat


def _pool_blockdiag(pool_w):
    k = len(POOL_WINDOWS)
    eye = jnp.eye(k, dtype=f32)
    w = pool_w[:, :, :, None, :] * eye[None, :, None, :, None]
    return w.reshape(DEPTH, MIX_W, MIX_W).astype(bf16)


def _time_major_state(s):
    return jnp.transpose(s, (0, 2, 1, 3))


def kernel(x_prompt, x_sample, state_pool, state_ssm_re, state_ssm_im, state_shortconv, state_conformer,
           norm1_g, norm2_g, final_g, w_in, b_gate, pool_w, pool_scale, lam_re, lam_im, log_dt,
           b_re, b_im, c_re, c_im, d_skip, w_glu, b_glu, sc_w, cf_w, cf_ln_g, cf_ln_b,
           w_branch, w_out, w_ffn_in, w_ffn_out):
    nb = x_prompt.shape[0]
    lam, bmat, cmat = _ssm_params(lam_re, lam_im, log_dt, b_re, b_im, c_re, c_im)
    row = lambda a: a.reshape(DEPTH, 1, a.shape[-1])
    taps = lambda w: jnp.broadcast_to(w[:, :, None, :], w.shape[:2] + (SUBLANES, MIX_W))
    mixer_w = (row(norm1_g), w_in, b_gate, _pool_blockdiag(pool_w), row(pool_scale),
               lam, bmat, cmat, row(d_skip), w_glu.astype(bf16), row(b_glu), taps(sc_w), taps(cf_w),
               row(cf_ln_g), row(cf_ln_b), w_branch.reshape(DEPTH, N_BRANCH * MIX_W, D_MODEL), w_out)
    n2 = row(norm2_g)
    wfi, wfo = w_ffn_in, w_ffn_out
    fg = final_g.reshape(1, D_MODEL)

    groups = []
    groups.append(dict(x=x_prompt, seq_major=True, pos0=0, tt=64, nsb=nb, st=None))
    ns = x_sample.shape[0]
    groups.append(dict(x=jnp.transpose(x_sample, (1, 0, 2)), seq_major=False, pos0=PAST_LEN,
                       tt=x_sample.shape[1], nsb=64,
                       st=(_time_major_state(state_pool), state_ssm_re.reshape(DEPTH, ns, SSM_W),
                           state_ssm_im.reshape(DEPTH, ns, SSM_W), _time_major_state(state_shortconv),
                           _time_major_state(state_conformer))))

    results = []
    cast_mixer, cast_ffn = [None] * DEPTH, [None] * DEPTH
    for grp in groups:
        x3, seq_major = grp["x"], grp["seq_major"]
        outs = [[] for _ in range(5)]
        for layer in range(DEPTH):
            last = layer == DEPTH - 1
            res = _mixer_call(layer, x3, grp["st"], mixer_w, tt=grp["tt"], nsb=grp["nsb"], pos0=grp["pos0"],
                              seq_major_in=seq_major and layer == 0, bf16_weights=cast_mixer[layer])
            x3, po, reo, imo, sco, cfo = res[:6]
            if cast_mixer[layer] is None:
                cast_mixer[layer] = tuple(res[6:])
            t_total, n_seq, _ = x3.shape
            for lst, o in zip(outs, (po, reo, imo, sco, cfo)):
                lst.append(o)
            x2, cast = _ffn_call(layer, x3.reshape(t_total * n_seq, D_MODEL), n2, wfi, wfo, fg, rows=FFN_ROWS,
                                 final=last, seq_major_out=n_seq if (seq_major and last) else 0,
                                 bf16_weights=cast_ffn[layer])
            if cast_ffn[layer] is None:
                cast_ffn[layer] = cast
            x3 = x2 if (seq_major and last) else x2.reshape(t_total, n_seq, D_MODEL)
        y = x3 if seq_major else jnp.transpose(x3, (1, 0, 2))
        pool = jnp.transpose(jnp.stack(outs[0]), (0, 2, 1, 3))
        sre = jnp.stack(outs[1]).reshape(DEPTH, n_seq, SSM_GROUPS, SSM_STATE)
        sim = jnp.stack(outs[2]).reshape(DEPTH, n_seq, SSM_GROUPS, SSM_STATE)
        sc = jnp.transpose(jnp.stack(outs[3]), (0, 2, 1, 3))
        cf = jnp.transpose(jnp.stack(outs[4]), (0, 2, 1, 3))
        results.append((y, pool, sre, sim, sc, cf))

    (yp, pp, rp, ip, cp, fp), (ys, ps, rs, is_, cs, fs) = results
    return (yp, ys, pp, ps, rp, rs, ip, is_, cp, cs, fp, fs)
```
